```python
import math
import jax, jax.numpy as jnp
from jax import lax
import numpy as np

D_MODEL = 1024
BATCH = 16
SEQ = 2048
DEPTH = 1

HEAD_DIM = 64
DIFF_HEADS = 4
DIFF_VDIM = 2 * HEAD_DIM
DIL_HEADS = 8
DIL_CONFIGS = ((128, 1), (512, 4), (2048, 16))
DIFF_WIDTH = DIFF_HEADS * DIFF_VDIM
DIL_WIDTH = DIL_HEADS * HEAD_DIM
MIX_WIDTH = DIFF_WIDTH + DIL_WIDTH
DIFF_QK_COLS = DIFF_HEADS * 2 * HEAD_DIM
IN_COLS = 2 * DIFF_QK_COLS + DIFF_WIDTH + 3 * DIL_WIDTH
D_FF = 2816
ROPE_THETA = 500000.0
ROPE_DIM = HEAD_DIM // 4
BLOCK = 128
EPS = 1e-5

kernel_name = "hymba_diff_dilated_macaron"


def rmsnorm(x, g):
    xf = x.astype(jnp.float32)
    y = xf * lax.rsqrt(jnp.mean(xf * xf, axis=-1, keepdims=True) + EPS)
    return (y * g.astype(jnp.float32)).astype(x.dtype)


def swiglu(x, w_gate, w_up, w_down):
    return (jax.nn.silu(x @ w_gate) * (x @ w_up)) @ w_down


def rope_partial(x, positions):
    half = ROPE_DIM // 2
    inv = jnp.exp(-math.log(ROPE_THETA) * jnp.arange(half, dtype=jnp.float32) * 2.0 / ROPE_DIM)
    ang = positions.astype(jnp.float32)[:, :, None, None] * inv
    cos, sin = jnp.cos(ang), jnp.sin(ang)
    xf = x.astype(jnp.float32)
    x1, x2 = xf[..., :half], xf[..., half:ROPE_DIM]
    out = jnp.concatenate([x1 * cos - x2 * sin, x2 * cos + x1 * sin, xf[..., ROPE_DIM:]], axis=-1)
    return out.astype(x.dtype)


def diff_attention(q, k, v, lam):
    B, S = q.shape[0], q.shape[1]
    nb = S // BLOCK
    scale = HEAD_DIM ** -0.5
    qb = q.reshape(B, nb, BLOCK, *q.shape[2:]).transpose(1, 0, 2, 3, 4, 5)
    kpos = jnp.arange(S)

    def one_block(args):
        qblk, bi = args
        s = jnp.einsum('bqhcd,bkhcd->bhcqk', qblk, k).astype(jnp.float32) * scale
        qpos = bi * BLOCK + jnp.arange(BLOCK)
        mask = kpos[None, :] <= qpos[:, None]
        p = jax.nn.softmax(jnp.where(mask, s, -jnp.inf), axis=-1)
        attn = p[:, :, 0] - lam * p[:, :, 1]
        return jnp.einsum('bhqk,bkhe->bqhe', attn.astype(v.dtype), v)

    out = lax.map(one_block, (qb, jnp.arange(nb)))
    return out.transpose(1, 0, 2, 3, 4).reshape(B, S, q.shape[2], v.shape[-1])


def dilated_branch(q, k, v, window, dilation):
    B, S, H, dh = q.shape
    L = S // dilation
    w = window // dilation
    Lp = -(-L // BLOCK) * BLOCK
    nb = Lp // BLOCK
    n_prev = -(-w // BLOCK)
    KB = (n_prev + 1) * BLOCK
    scale = dh ** -0.5

    def to_sub(t):
        return t.reshape(B, L, dilation, H, dh).transpose(0, 2, 3, 1, 4)

    qs = jnp.pad(to_sub(q), ((0, 0), (0, 0), (0, 0), (0, Lp - L), (0, 0)))
    qb = qs.reshape(B, dilation, H, nb, BLOCK, dh)

    def band(t):
        tp = jnp.pad(to_sub(t), ((0, 0), (0, 0), (0, 0), (n_prev * BLOCK, Lp - L), (0, 0)))
        tb = tp.reshape(B, dilation, H, nb + n_prev, BLOCK, dh)
        return jnp.concatenate([tb[:, :, :, j:j + nb] for j in range(n_prev + 1)], axis=4)

    kb, vb = band(k), band(v)
    s = jnp.einsum('brhnqd,brhnkd->brhnqk', qb, kb).astype(jnp.float32) * scale
    q_off = jnp.arange(BLOCK)[:, None]
    k_off = jnp.arange(KB)[None, :]
    dist = q_off + n_prev * BLOCK - k_off
    kpos = jnp.arange(nb)[:, None, None] * BLOCK - n_prev * BLOCK + k_off[None]
    valid = (dist >= 0) & (dist <= w) & (kpos >= 0)
    s = jnp.where(valid, s, -jnp.inf)
    m = jnp.max(s, axis=-1, keepdims=True)
    p = jnp.exp(s - m)
    denom = jnp.sum(p, axis=-1)
    o = jnp.einsum('brhnqk,brhnkd->brhnqd', p.astype(v.dtype), vb).astype(jnp.float32) / denom[..., None]
    lse = m[..., 0] + jnp.log(denom)
    o = o.reshape(B, dilation, H, Lp, dh)[:, :, :, :L].transpose(0, 3, 1, 2, 4).reshape(B, S, H, dh)
    lse = lse.reshape(B, dilation, H, Lp)[..., :L].transpose(0, 3, 1, 2).reshape(B, S, H)
    return o, lse


def dilated_attention(q, k, v):
    outs, lses = [], []
    for window, dilation in DIL_CONFIGS:
        o, l = dilated_branch(q, k, v, window, dilation)
        outs.append(o)
        lses.append(l)
    wts = jax.nn.softmax(jnp.stack(lses, axis=0), axis=0)
    out = jnp.sum(wts[..., None] * jnp.stack(outs, axis=0), axis=0)
    return out.astype(q.dtype)


def setup_inputs(seed: int = 0) -> dict:
    key = jax.random.key(seed)
    ks = jax.random.split(key, 20)
    f32 = jnp.float32

    def nrm(k, shape, scale):
        return jax.random.normal(k, shape, f32) * scale

    def gain(k, shape):
        return 1.0 + 0.02 * jax.random.normal(k, shape, f32)

    return {
        "x": jax.random.normal(ks[0], (BATCH, SEQ, D_MODEL), f32),
        "positions": jnp.broadcast_to(jnp.arange(SEQ, dtype=jnp.int32), (BATCH, SEQ)),
        "ffn1_norm": gain(ks[1], (DEPTH, D_MODEL)),
        "ffn1_gate": nrm(ks[2], (DEPTH, D_MODEL, D_FF), D_MODEL ** -0.5),
        "ffn1_up": nrm(ks[3], (DEPTH, D_MODEL, D_FF), D_MODEL ** -0.5),
        "ffn1_down": nrm(ks[4], (DEPTH, D_FF, D_MODEL), D_FF ** -0.5),
        "mix_norm": gain(ks[5], (DEPTH, D_MODEL)),
        "w_in": nrm(ks[6], (DEPTH, D_MODEL, IN_COLS), D_MODEL ** -0.5),
        "lambda_q1": nrm(ks[7], (DEPTH, HEAD_DIM), 0.1),
        "lambda_k1": nrm(ks[8], (DEPTH, HEAD_DIM), 0.1),
        "lambda_q2": nrm(ks[9], (DEPTH, HEAD_DIM), 0.1),
        "lambda_k2": nrm(ks[10], (DEPTH, HEAD_DIM), 0.1),
        "subln_gain": gain(ks[11], (DEPTH, DIFF_VDIM)),
        "w_out": nrm(ks[12], (DEPTH, MIX_WIDTH, D_MODEL), MIX_WIDTH ** -0.5),
        "ffn2_norm": gain(ks[13], (DEPTH, D_MODEL)),
        "ffn2_gate": nrm(ks[14], (DEPTH, D_MODEL, D_FF), D_MODEL ** -0.5),
        "ffn2_up": nrm(ks[15], (DEPTH, D_MODEL, D_FF), D_MODEL ** -0.5),
        "ffn2_down": nrm(ks[16], (DEPTH, D_FF, D_MODEL), D_FF ** -0.5),
        "final_norm": gain(ks[17], (D_MODEL,)),
    }


def reference(x, positions, ffn1_norm, ffn1_gate, ffn1_up, ffn1_down, mix_norm, w_in,
              lambda_q1, lambda_k1, lambda_q2, lambda_k2, subln_gain, w_out,
              ffn2_norm, ffn2_gate, ffn2_up, ffn2_down, final_norm):
    B, S, _ = x.shape
    for l in range(DEPTH):
        x = x + 0.5 * swiglu(rmsnorm(x, ffn1_norm[l]), ffn1_gate[l], ffn1_up[l], ffn1_down[l])

        h = rmsnorm(x, mix_norm[l])
        proj = h @ w_in[l]
        splits = np.cumsum([DIFF_QK_COLS, DIFF_QK_COLS, DIFF_WIDTH, DIL_WIDTH, DIL_WIDTH]).tolist()
        dq, dk, dv, gq, gk, gv = jnp.split(proj, splits, axis=-1)

        lambda_init = 0.8 - 0.6 * math.exp(-0.3 * l)
        lam = (jnp.exp(jnp.sum(lambda_q1[l].astype(jnp.float32) * lambda_k1[l].astype(jnp.float32)))
               - jnp.exp(jnp.sum(lambda_q2[l].astype(jnp.float32) * lambda_k2[l].astype(jnp.float32)))
               + lambda_init)
        dq = rope_partial(dq.reshape(B, S, DIFF_HEADS * 2, HEAD_DIM), positions).reshape(B, S, DIFF_HEADS, 2, HEAD_DIM)
        dk = rope_partial(dk.reshape(B, S, DIFF_HEADS * 2, HEAD_DIM), positions).reshape(B, S, DIFF_HEADS, 2, HEAD_DIM)
        dv = dv.reshape(B, S, DIFF_HEADS, DIFF_VDIM)
        a_out = diff_attention(dq, dk, dv, lam)
        a_out = rmsnorm(a_out, subln_gain[l]) * (1.0 - lambda_init)
        a_out = a_out.reshape(B, S, DIFF_WIDTH)

        gq = rope_partial(gq.reshape(B, S, DIL_HEADS, HEAD_DIM), positions)
        gk = rope_partial(gk.reshape(B, S, DIL_HEADS, HEAD_DIM), positions)
        gv = gv.reshape(B, S, DIL_HEADS, HEAD_DIM)
        b_out = dilated_attention(gq, gk, gv).reshape(B, S, DIL_WIDTH)

        x = x + jnp.concatenate([a_out, b_out], axis=-1) @ w_out[l]

        x = x + 0.5 * swiglu(rmsnorm(x, ffn2_norm[l]), ffn2_gate[l], ffn2_up[l], ffn2_down[l])
    return rmsnorm(x, final_norm)
```

```python
import functools
import math

import numpy as np
import jax
import jax.numpy as jnp
from jax import lax
from jax.experimental import pallas as pl
from jax.experimental.pallas import tpu as pltpu

HEAD_DIM = 64
DIFF_HEADS = 4
DIFF_VDIM = 2 * HEAD_DIM
DIL_HEADS = 8
DIL_CONFIGS = ((128, 1), (512, 4), (2048, 16))
DIFF_WIDTH = DIFF_HEADS * DIFF_VDIM
DIL_WIDTH = DIL_HEADS * HEAD_DIM
ROPE_THETA = 500000.0
ROPE_DIM = HEAD_DIM // 4
EPS = 1e-5

LANES = 128
ROW_TILE = 512
FF_CHUNK = 256
PROJ_CHUNK = 512
Q_TILE = 512
VMEM_LIMIT = 56 * 1024 * 1024

F32 = jnp.float32
BF16 = jnp.bfloat16


def _resident(shape):
    return pl.BlockSpec(shape, lambda *_: (0,) * len(shape), pipeline_mode=pl.Buffered(1))


def _rmsnorm(x, g):
    return x * lax.rsqrt(jnp.mean(x * x, axis=-1, keepdims=True) + EPS) * g


def _swiglu_half_step(x, g_ref, wg_ref, wu_ref, wd_ref, act_ref):
    h = _rmsnorm(x, g_ref[...]).astype(BF16)
    d_ff = wg_ref.shape[1]
    for c in range(d_ff // FF_CHUNK):
        cols = slice(c * FF_CHUNK, (c + 1) * FF_CHUNK)
        gate = jnp.dot(h, wg_ref[:, cols], preferred_element_type=F32)
        up = jnp.dot(h, wu_ref[:, cols], preferred_element_type=F32)
        act_ref[:, cols] = (gate * jax.nn.sigmoid(gate) * up).astype(BF16)
    y = jnp.dot(act_ref[...], wd_ref[...], preferred_element_type=F32)
    return x + 0.5 * y


def _ffn1_kernel(x_ref, g_ref, wg_ref, wu_ref, wd_ref, o_ref, act_ref):
    o_ref[...] = _swiglu_half_step(x_ref[...], g_ref, wg_ref, wu_ref, wd_ref, act_ref)


def _out_ffn2_kernel(x_ref, a_ref, b_ref, wo_ref, g_ref, wg_ref, wu_ref, wd_ref, gf_ref,
                     o_ref, act_ref, *, final_norm):
    wa = wo_ref[0:DIFF_WIDTH, :]
    wb = wo_ref[DIFF_WIDTH:DIFF_WIDTH + DIL_WIDTH, :]
    x = (x_ref[...]
         + jnp.dot(a_ref[...], wa, preferred_element_type=F32)
         + jnp.dot(b_ref[...], wb, preferred_element_type=F32))
    y = _swiglu_half_step(x, g_ref, wg_ref, wu_ref, wd_ref, act_ref)
    if final_norm:
        y = _rmsnorm(y, gf_ref[...])
    o_ref[...] = y


def _in_proj_kernel(x_ref, pos_ref, g_ref, w_ref, inv_ref, lo_ref, hi_ref, o_ref, *, rope_scale):
    h = _rmsnorm(x_ref[...], g_ref[...]).astype(BF16)
    ang = pos_ref[...].astype(F32) * inv_ref[...]
    cos = jnp.cos(ang)
    sin = jnp.sin(ang)
    sin_lo = sin * lo_ref[...]
    sin_hi = sin * hi_ref[...]
    half = ROPE_DIM // 2
    for c, scale in enumerate(rope_scale):
        cols = slice(c * PROJ_CHUNK, (c + 1) * PROJ_CHUNK)
        p = jnp.dot(h, w_ref[:, cols], preferred_element_type=F32)
        if scale is None:
            o_ref[:, cols] = p.astype(BF16)
            continue
        for s in range(PROJ_CHUNK // LANES):
            xs = p[:, s * LANES:(s + 1) * LANES]
            rot = (xs * cos
                   + pltpu.roll(xs, LANES - half, axis=1) * sin_lo
                   + pltpu.roll(xs, half, axis=1) * sin_hi)
            lo = c * PROJ_CHUNK + s * LANES
            o_ref[:, lo:lo + LANES] = (rot * scale).astype(BF16)


def _attn_kernel(q_ref, k_ref, v_ref, bias_ref, lq1_ref, lk1_ref, lq2_ref, lk2_ref, gain_ref,
                 o_ref, *, differential, lambda_init):
    seq = q_ref.shape[0]
    n_blocks = seq // Q_TILE
    lane = lax.broadcasted_iota(jnp.int32, (Q_TILE, LANES), 1)
    lo_half = lane < HEAD_DIM
    nt = (((1,), (1,)), ((), ()))

    if differential:
        lam = (jnp.exp(jnp.sum(lq1_ref[...] * lk1_ref[...], axis=-1, keepdims=True))
               - jnp.exp(jnp.sum(lq2_ref[...] * lk2_ref[...], axis=-1, keepdims=True))
               + lambda_init)

    for i in range(n_blocks):
        rows = slice(i * Q_TILE, (i + 1) * Q_TILE)
        n_off = i * Q_TILE
        q = q_ref[rows, :]
        k_diag = k_ref[rows, :]
        v_diag = v_ref[rows, :]
        bias_diag = bias_ref[:, seq - Q_TILE:seq]
        outs = []
        for q_half in (jnp.where(lo_half, q, 0), jnp.where(lo_half, 0, q)):
            s_d = lax.dot_general(q_half, k_diag, nt, preferred_element_type=F32) + bias_diag
            m = jnp.max(s_d, axis=-1, keepdims=True)
            if n_off:
                s_o = lax.dot_general(q_half, k_ref[0:n_off, :], nt, preferred_element_type=F32)
                if not differential:
                    s_o = s_o + bias_ref[:, seq - Q_TILE - n_off:seq - Q_TILE]
                m = jnp.maximum(m, jnp.max(s_o, axis=-1, keepdims=True))
            p_d = jnp.exp(s_d - m)
            denom = jnp.sum(p_d, axis=-1, keepdims=True)
            acc = jnp.dot(p_d.astype(BF16), v_diag, preferred_element_type=F32)
            if n_off:
                p_o = jnp.exp(s_o - m)
                denom = denom + jnp.sum(p_o, axis=-1, keepdims=True)
                acc = acc + jnp.dot(p_o.astype(BF16), v_ref[0:n_off, :], preferred_element_type=F32)
            outs.append(acc * (1.0 / denom))
        if differential:
            o = outs[0] - lam * outs[1]
            o = _rmsnorm(o, gain_ref[...]) * (1.0 - lambda_init)
        else:
            o = jnp.where(lo_half, outs[0], outs[1])
        o_ref[rows, :] = o.astype(BF16)


def _bias_table(seq, branches):
    r = np.arange(Q_TILE)[:, None]
    u = np.arange(seq)[None, :]
    d = seq - Q_TILE + r - u
    count = np.zeros(d.shape, np.float64)
    for span, stride in branches:
        count += (d >= 0) & (d <= span) & (d % stride == 0)
    with np.errstate(divide="ignore"):
        return jnp.asarray(np.log(count), dtype=F32)


def _rope_lane_tables():
    half = ROPE_DIM // 2
    inv = jnp.exp(-math.log(ROPE_THETA) * jnp.arange(half, dtype=F32) * 2.0 / ROPE_DIM)
    d = np.arange(LANES) % HEAD_DIM
    inv_lane = jnp.where(d < ROPE_DIM, jnp.tile(inv, LANES // half), 0.0)
    lo = np.where(d < half, -1.0, 0.0)
    hi = np.where((d >= half) & (d < ROPE_DIM), 1.0, 0.0)
    as_row = lambda t: jnp.asarray(t, F32).reshape(1, LANES)
    return as_row(inv_lane), as_row(lo), as_row(hi)


def _row_spec(cols):
    return pl.BlockSpec((ROW_TILE, cols), lambda i: (i, 0))


def _params(*semantics):
    return pltpu.CompilerParams(dimension_semantics=semantics, vmem_limit_bytes=VMEM_LIMIT)


def _ffn1(x, g, wg, wu, wd):
    t, d = x.shape
    f = wg.shape[1]
    return pl.pallas_call(
        _ffn1_kernel,
        grid=(t // ROW_TILE,),
        in_specs=[_row_spec(d), _resident((1, d)), _resident((d, f)), _resident((d, f)),
                  _resident((f, d))],
        out_specs=_row_spec(d),
        out_shape=jax.ShapeDtypeStruct((t, d), F32),
        scratch_shapes=[pltpu.VMEM((ROW_TILE, f), BF16)],
        compiler_params=_params("parallel"),
        name="ffn1",
    )(x, g, wg, wu, wd)


def _in_proj(x, pos, g, w, rope_tables, rope_scale):
    t, d = x.shape
    n = w.shape[1]
    return pl.pallas_call(
        functools.partial(_in_proj_kernel, rope_scale=rope_scale),
        grid=(t // ROW_TILE,),
        in_specs=[_row_spec(d), _row_spec(1), _resident((1, d)), _resident((d, n)),
                  _resident((1, LANES)), _resident((1, LANES)), _resident((1, LANES))],
        out_specs=_row_spec(n),
        out_shape=jax.ShapeDtypeStruct((t, n), BF16),
        compiler_params=_params("parallel"),
        name="in_proj",
    )(x, pos, g, w, *rope_tables)


def _attention(proj, bias, lams, gain, *, q_col, k_col, v_col, groups, differential, lambda_init):
    b, s, _ = proj.shape
    col_spec = lambda first: pl.BlockSpec((None, s, LANES), lambda bi, gi: (bi, 0, first + gi))
    return pl.pallas_call(
        functools.partial(_attn_kernel, differential=differential, lambda_init=lambda_init),
        grid=(b, groups),
        in_specs=[col_spec(q_col), col_spec(k_col), col_spec(v_col), _resident(bias.shape)]
                 + [_resident((1, HEAD_DIM))] * 4 + [_resident((1, LANES))],
        out_specs=pl.BlockSpec((None, s, LANES), lambda bi, gi: (bi, 0, gi)),
        out_shape=jax.ShapeDtypeStruct((b, s, groups * LANES), BF16),
        compiler_params=_params("parallel", "parallel"),
        name="diff_attn" if differential else "dil_attn",
    )(proj, proj, proj, bias, *lams, gain)


def _out_ffn2(x, a, b, wo, g, wg, wu, wd, gf, final_norm):
    t, d = x.shape
    f = wg.shape[1]
    return pl.pallas_call(
        functools.partial(_out_ffn2_kernel, final_norm=final_norm),
        grid=(t // ROW_TILE,),
        in_specs=[_row_spec(d), _row_spec(a.shape[1]), _row_spec(b.shape[1]),
                  _resident(wo.shape), _resident((1, d)), _resident((d, f)), _resident((d, f)),
                  _resident((f, d)), _resident((1, d))],
        out_specs=_row_spec(d),
        out_shape=jax.ShapeDtypeStruct((t, d), F32),
        scratch_shapes=[pltpu.VMEM((ROW_TILE, f), BF16)],
        compiler_params=_params("parallel"),
        name="out_ffn2",
    )(x, a, b, wo, g, wg, wu, wd, gf)


def kernel(x, positions, ffn1_norm, ffn1_gate, ffn1_up, ffn1_down, mix_norm, w_in,
           lambda_q1, lambda_k1, lambda_q2, lambda_k2, subln_gain, w_out,
           ffn2_norm, ffn2_gate, ffn2_up, ffn2_down, final_norm):
    bsz, seq, d = x.shape
    depth = ffn1_norm.shape[0]
    t = bsz * seq
    assert t % ROW_TILE == 0 and seq % Q_TILE == 0
    assert w_in.shape[2] == 3 * (DIFF_WIDTH + DIL_WIDTH) and DIFF_WIDTH % PROJ_CHUNK == 0

    row = lambda v: v.astype(F32).reshape(1, -1)
    q_scale = HEAD_DIM ** -0.5
    rope_scale = (q_scale, 1.0, None, q_scale, 1.0, None)
    rope_scale = tuple(s for s in rope_scale for _ in range(DIFF_WIDTH // PROJ_CHUNK))
    groups = DIFF_WIDTH // LANES
    rope_tables = _rope_lane_tables()
    causal_bias = _bias_table(seq, ((seq, 1),))
    dilated_bias = _bias_table(seq, tuple((w, dil) for w, dil in DIL_CONFIGS))
    pos = positions.reshape(t, 1)

    h = x.reshape(t, d)
    for l in range(depth):
        lambda_init = 0.8 - 0.6 * math.exp(-0.3 * l)
        h = _ffn1(h, row(ffn1_norm[l]), ffn1_gate[l].astype(BF16), ffn1_up[l].astype(BF16),
                  ffn1_down[l].astype(BF16))
        proj = _in_proj(h, pos, row(mix_norm[l]), w_in[l].astype(BF16), rope_tables, rope_scale)
        proj = proj.reshape(bsz, seq, -1)
        lams = (row(lambda_q1[l]), row(lambda_k1[l]), row(lambda_q2[l]), row(lambda_k2[l]))
        gain = row(subln_gain[l])
        a = _attention(proj, causal_bias, lams, gain, q_col=0, k_col=groups, v_col=2 * groups,
                       groups=groups, differential=True, lambda_init=lambda_init)
        b = _attention(proj, dilated_bias, lams, gain, q_col=3 * groups, k_col=4 * groups,
                       v_col=5 * groups, groups=groups, differential=False,
                       lambda_init=lambda_init)
        h = _out_ffn2(h, a.reshape(t, -1), b.reshape(t, -1), w_out[l].astype(BF16),
                      row(ffn2_norm[l]), ffn2_gate[l].astype(BF16), ffn2_up[l].astype(BF16),
                      ffn2_down[l].astype(BF16), row(final_norm), final_norm=(l == depth - 1))
    return h.reshape(bsz, seq, d)
```

```python
import functools
import math

import numpy as np
import jax
import jax.numpy as jnp
from jax import lax
from jax.experimental import pallas as pl
from jax.experimental.pallas import tpu as pltpu

HEAD_DIM = 64
DIFF_HEADS = 4
DIFF_VDIM = 2 * HEAD_DIM
DIL_HEADS = 8
DIL_CONFIGS = ((128, 1), (512, 4), (2048, 16))
DIFF_WIDTH = DIFF_HEADS * DIFF_VDIM
DIL_WIDTH = DIL_HEADS * HEAD_DIM
ROPE_THETA = 500000.0
ROPE_DIM = HEAD_DIM // 4
EPS = 1e-5

LANES = 128
ROW_TILE = 512
FF_CHUNK = 256
PROJ_CHUNK = 512
Q_TILE = 256
ROW_GROUP = 128
VMEM_LIMIT = 56 * 1024 * 1024

F32 = jnp.float32
BF16 = jnp.bfloat16
NT_DIMS = (((1,), (1,)), ((), ()))


def _resident(shape):
    return pl.BlockSpec(shape, lambda *_: (0,) * len(shape), pipeline_mode=pl.Buffered(1))


def _rmsnorm(x, g):
    return x * lax.rsqrt(jnp.mean(x * x, axis=-1, keepdims=True) + EPS) * g


def _swiglu_half_step(x, g_ref, wg_ref, wu_ref, wd_ref, act_ref):
    h = _rmsnorm(x, g_ref[...]).astype(BF16)
    d_ff = wg_ref.shape[1]
    for c in range(d_ff // FF_CHUNK):
        cols = slice(c * FF_CHUNK, (c + 1) * FF_CHUNK)
        gate = jnp.dot(h, wg_ref[:, cols], preferred_element_type=F32)
        up = jnp.dot(h, wu_ref[:, cols], preferred_element_type=F32)
        act_ref[:, cols] = (gate * jax.nn.sigmoid(gate) * up).astype(BF16)
    y = jnp.dot(act_ref[...], wd_ref[...], preferred_element_type=F32)
    return x + 0.5 * y


def _ffn1_kernel(x_ref, g_ref, wg_ref, wu_ref, wd_ref, o_ref, act_ref):
    o_ref[...] = _swiglu_half_step(x_ref[...], g_ref, wg_ref, wu_ref, wd_ref, act_ref)


def _out_ffn2_kernel(x_ref, a_ref, b_ref, wo_ref, g_ref, wg_ref, wu_ref, wd_ref, gf_ref,
                     o_ref, act_ref, *, final_norm):
    wa = wo_ref[0:DIFF_WIDTH, :]
    wb = wo_ref[DIFF_WIDTH:DIFF_WIDTH + DIL_WIDTH, :]
    x = (x_ref[...]
         + jnp.dot(a_ref[...], wa, preferred_element_type=F32)
         + jnp.dot(b_ref[...], wb, preferred_element_type=F32))
    y = _swiglu_half_step(x, g_ref, wg_ref, wu_ref, wd_ref, act_ref)
    if final_norm:
        y = _rmsnorm(y, gf_ref[...])
    o_ref[...] = y


def _in_proj_kernel(x_ref, pos_ref, g_ref, w_ref, inv_ref, lo_ref, hi_ref, o_ref, *, rope_scale):
    h = _rmsnorm(x_ref[...], g_ref[...]).astype(BF16)
    ang = pos_ref[...].astype(F32) * inv_ref[...]
    cos = jnp.cos(ang)
    sin = jnp.sin(ang)
    sin_lo = sin * lo_ref[...]
    sin_hi = sin * hi_ref[...]
    half = ROPE_DIM // 2
    for c, scale in enumerate(rope_scale):
        cols = slice(c * PROJ_CHUNK, (c + 1) * PROJ_CHUNK)
        p = jnp.dot(h, w_ref[:, cols], preferred_element_type=F32)
        if scale is None:
            o_ref[:, cols] = p.astype(BF16)
            continue
        for s in range(PROJ_CHUNK // LANES):
            xs = p[:, s * LANES:(s + 1) * LANES]
            rot = (xs * cos
                   + pltpu.roll(xs, LANES - half, axis=1) * sin_lo
                   + pltpu.roll(xs, half, axis=1) * sin_hi)
            lo = c * PROJ_CHUNK + s * LANES
            o_ref[:, lo:lo + LANES] = (rot * scale).astype(BF16)


def _interleave(work):
    total = max(n for _, n in work)
    done = [0] * len(work)
    for tick in range(1, total + 1):
        for idx, (gen, n) in enumerate(work):
            want = -(-tick * n // total)
            while done[idx] < want:
                next(gen, None)
                done[idx] += 1
    for gen, _ in work:
        for _ in gen:
            pass


def _attention_pipeline(q_ref, k_ref, v_ref, bias_ref, o_ref, s_buf, p_buf, l_buf, finalize):
    seq = q_ref.shape[0]
    n_blocks = seq // Q_TILE
    width = bias_ref.shape[1]
    lane = lax.broadcasted_iota(jnp.int32, (Q_TILE, LANES), 1)
    lo_half = lane < HEAD_DIM

    def scores(t):
        q = q_ref[t * Q_TILE:(t + 1) * Q_TILE, :]
        q2 = jnp.concatenate([jnp.where(lo_half, q, 0), jnp.where(lo_half, 0, q)], axis=0)
        for j in range(t + 1):
            cols = slice(j * Q_TILE, (j + 1) * Q_TILE)
            s = lax.dot_general(q2, k_ref[cols, :], NT_DIMS, preferred_element_type=F32)
            first = width - (t + 1 - j) * Q_TILE
            if first >= 0:
                s = s + bias_ref[:, first:first + Q_TILE]
            s_buf[t % 2][:, cols] = s
            yield

    def softmax(t):
        n_keys = (t + 1) * Q_TILE
        s_scr, p_scr, l_scr = s_buf[t % 2], p_buf[t % 2], l_buf[t % 2]
        for g in range(2 * Q_TILE // ROW_GROUP):
            rows = slice(g * ROW_GROUP, (g + 1) * ROW_GROUP)
            m_wide = s_scr[rows, 0:LANES]
            for c in range(1, n_keys // LANES):
                m_wide = jnp.maximum(m_wide, s_scr[rows, c * LANES:(c + 1) * LANES])
            yield
            m = jnp.broadcast_to(jnp.max(m_wide, axis=-1, keepdims=True), (ROW_GROUP, LANES))
            l_wide = None
            for c in range(n_keys // LANES):
                cols = slice(c * LANES, (c + 1) * LANES)
                p = jnp.exp2(s_scr[rows, cols] - m)
                l_wide = p if l_wide is None else l_wide + p
                p_scr[rows, cols] = p.astype(BF16)
                if c % 4 == 3:
                    yield
            l_scr[rows, :] = jnp.broadcast_to(jnp.sum(l_wide, axis=-1, keepdims=True),
                                              (ROW_GROUP, LANES))

    def values(t):
        n_keys = (t + 1) * Q_TILE
        acc = jnp.dot(p_buf[t % 2][:, 0:n_keys], v_ref[0:n_keys, :], preferred_element_type=F32)
        out = acc * (1.0 / l_buf[t % 2][...])
        o = finalize(out[0:Q_TILE], out[Q_TILE:2 * Q_TILE])
        o_ref[t * Q_TILE:(t + 1) * Q_TILE, :] = o.astype(BF16)
        yield

    for step in range(n_blocks + 2):
        work = []
        if step < n_blocks:
            work.append((scores(step), step + 1))
        if 0 <= step - 1 < n_blocks:
            slabs = step * Q_TILE // LANES
            work.append((softmax(step - 1), (2 * Q_TILE // ROW_GROUP) * (1 + slabs // 4)))
        if 0 <= step - 2 < n_blocks:
            work.append((values(step - 2), 1))
        _interleave(work)


def _diff_attn_kernel(q_ref, k_ref, v_ref, bias_ref, lq1_ref, lk1_ref, lq2_ref, lk2_ref, gain_ref,
                      o_ref, s0, s1, p0, p1, l0, l1, *, lambda_init):
    lam = (jnp.exp(jnp.sum(lq1_ref[...] * lk1_ref[...], axis=-1, keepdims=True))
           - jnp.exp(jnp.sum(lq2_ref[...] * lk2_ref[...], axis=-1, keepdims=True))
           + lambda_init)

    def finalize(lo, hi):
        return _rmsnorm(lo - lam * hi, gain_ref[...]) * (1.0 - lambda_init)

    _attention_pipeline(q_ref, k_ref, v_ref, bias_ref, o_ref, (s0, s1), (p0, p1), (l0, l1), finalize)


def _dil_attn_kernel(q_ref, k_ref, v_ref, bias_ref, o_ref, s0, s1, p0, p1, l0, l1):
    lo_half = lax.broadcasted_iota(jnp.int32, (Q_TILE, LANES), 1) < HEAD_DIM

    def finalize(lo, hi):
        return jnp.where(lo_half, lo, hi)

    _attention_pipeline(q_ref, k_ref, v_ref, bias_ref, o_ref, (s0, s1), (p0, p1), (l0, l1), finalize)


def _bias_table(width, branches):
    r = np.arange(Q_TILE)[:, None]
    u = np.arange(width)[None, :]
    d = width - Q_TILE + r - u
    count = np.zeros(d.shape, np.float64)
    for span, stride in branches:
        count += (d >= 0) & (d <= span) & (d % stride == 0)
    with np.errstate(divide="ignore"):
        table = np.log2(count)
    return jnp.asarray(np.concatenate([table, table], axis=0), dtype=F32)


def _rope_lane_tables():
    half = ROPE_DIM // 2
    inv = jnp.exp(-math.log(ROPE_THETA) * jnp.arange(half, dtype=F32) * 2.0 / ROPE_DIM)
    d = np.arange(LANES) % HEAD_DIM
    inv_lane = jnp.where(d < ROPE_DIM, jnp.tile(inv, LANES // half), 0.0)
    lo = np.where(d < half, -1.0, 0.0)
    hi = np.where((d >= half) & (d < ROPE_DIM), 1.0, 0.0)
    as_row = lambda t: jnp.asarray(t, F32).reshape(1, LANES)
    return as_row(inv_lane), as_row(lo), as_row(hi)


def _row_spec(cols):
    return pl.BlockSpec((ROW_TILE, cols), lambda i: (i, 0))


def _params(*semantics):
    return pltpu.CompilerParams(dimension_semantics=semantics, vmem_limit_bytes=VMEM_LIMIT)


def _ffn1(x, g, wg, wu, wd):
    t, d = x.shape
    f = wg.shape[1]
    return pl.pallas_call(
        _ffn1_kernel,
        grid=(t // ROW_TILE,),
        in_specs=[_row_spec(d), _resident((1, d)), _resident((d, f)), _resident((d, f)),
                  _resident((f, d))],
        out_specs=_row_spec(d),
        out_shape=jax.ShapeDtypeStruct((t, d), F32),
        scratch_shapes=[pltpu.VMEM((ROW_TILE, f), BF16)],
        compiler_params=_params("parallel"),
        name="ffn1",
    )(x, g, wg, wu, wd)


def _in_proj(x, pos, g, w, rope_tables, rope_scale):
    t, d = x.shape
    n = w.shape[1]
    return pl.pallas_call(
        functools.partial(_in_proj_kernel, rope_scale=rope_scale),
        grid=(t // ROW_TILE,),
        in_specs=[_row_spec(d), _row_spec(1), _resident((1, d)), _resident((d, n)),
                  _resident((1, LANES)), _resident((1, LANES)), _resident((1, LANES))],
        out_specs=_row_spec(n),
        out_shape=jax.ShapeDtypeStruct((t, n), BF16),
        compiler_params=_params("parallel"),
        name="in_proj",
    )(x, pos, g, w, *rope_tables)


def _attention(body, name, proj, bias, extra, *, q_col, k_col, v_col, groups):
    b, s, _ = proj.shape
    col_spec = lambda first: pl.BlockSpec((None, s, LANES), lambda bi, gi: (bi, 0, first + gi))
    stacked = 2 * Q_TILE
    return pl.pallas_call(
        body,
        grid=(b, groups),
        in_specs=[col_spec(q_col), col_spec(k_col), col_spec(v_col), _resident(bias.shape)]
                 + [_resident(e.shape) for e in extra],
        out_specs=pl.BlockSpec((None, s, LANES), lambda bi, gi: (bi, 0, gi)),
        out_shape=jax.ShapeDtypeStruct((b, s, groups * LANES), BF16),
        scratch_shapes=[pltpu.VMEM((stacked, s), F32)] * 2 + [pltpu.VMEM((stacked, s), BF16)] * 2
                       + [pltpu.VMEM((stacked, LANES), F32)] * 2,
        compiler_params=_params("parallel", "parallel"),
        name=name,
    )(proj, proj, proj, bias, *extra)


def _out_ffn2(x, a, b, wo, g, wg, wu, wd, gf, final_norm):
    t, d = x.shape
    f = wg.shape[1]
    return pl.pallas_call(
        functools.partial(_out_ffn2_kernel, final_norm=final_norm),
        grid=(t // ROW_TILE,),
        in_specs=[_row_spec(d), _row_spec(a.shape[1]), _row_spec(b.shape[1]),
                  _resident(wo.shape), _resident((1, d)), _resident((d, f)), _resident((d, f)),
                  _resident((f, d)), _resident((1, d))],
        out_specs=_row_spec(d),
        out_shape=jax.ShapeDtypeStruct((t, d), F32),
        scratch_shapes=[pltpu.VMEM((ROW_TILE, f), BF16)],
        compiler_params=_params("parallel"),
        name="out_ffn2",
    )(x, a, b, wo, g, wg, wu, wd, gf)


def kernel(x, positions, ffn1_norm, ffn1_gate, ffn1_up, ffn1_down, mix_norm, w_in,
           lambda_q1, lambda_k1, lambda_q2, lambda_k2, subln_gain, w_out,
           ffn2_norm, ffn2_gate, ffn2_up, ffn2_down, final_norm):
    bsz, seq, d = x.shape
    depth = ffn1_norm.shape[0]
    t = bsz * seq
    assert t % ROW_TILE == 0 and seq % Q_TILE == 0
    assert w_in.shape[2] == 3 * (DIFF_WIDTH + DIL_WIDTH) and DIFF_WIDTH % PROJ_CHUNK == 0

    row = lambda v: v.astype(F32).reshape(1, -1)
    q_scale = HEAD_DIM ** -0.5 * math.log2(math.e)
    rope_scale = (q_scale, 1.0, None, q_scale, 1.0, None)
    rope_scale = tuple(s for s in rope_scale for _ in range(DIFF_WIDTH // PROJ_CHUNK))
    groups = DIFF_WIDTH // LANES
    rope_tables = _rope_lane_tables()
    causal_bias = _bias_table(Q_TILE, ((seq, 1),))
    dilated_bias = _bias_table(seq, DIL_CONFIGS)
    pos = positions.reshape(t, 1)

    h = x.reshape(t, d)
    for l in range(depth):
        lambda_init = 0.8 - 0.6 * math.exp(-0.3 * l)
        h = _ffn1(h, row(ffn1_norm[l]), ffn1_gate[l].astype(BF16), ffn1_up[l].astype(BF16),
                  ffn1_down[l].astype(BF16))
        proj = _in_proj(h, pos, row(mix_norm[l]), w_in[l].astype(BF16), rope_tables, rope_scale)
        proj = proj.reshape(bsz, seq, -1)
        lams = (row(lambda_q1[l]), row(lambda_k1[l]), row(lambda_q2[l]), row(lambda_k2[l]))
        a = _attention(functools.partial(_diff_attn_kernel, lambda_init=lambda_init), "diff_attn",
                       proj, causal_bias, (*lams, row(subln_gain[l])),
                       q_col=0, k_col=groups, v_col=2 * groups, groups=groups)
        b = _attention(_dil_attn_kernel, "dil_attn", proj, dilated_bias, (),
                       q_col=3 * groups, k_col=4 * groups, v_col=5 * groups, groups=groups)
        h = _out_ffn2(h, a.reshape(t, -1), b.reshape(t, -1), w_out[l].astype(BF16),
                      row(ffn2_norm[l]), ffn2_gate[l].astype(BF16), ffn2_up[l].astype(BF16),
                      ffn2_down[l].astype(BF16), row(final_norm), final_norm=(l == depth - 1))
    return h.reshape(bsz, seq, d)
```

```python
import functools
import math

import numpy as np
import jax
import jax.numpy as jnp
from jax import lax
from jax.experimental import pallas as pl
from jax.experimental.pallas import tpu as pltpu

HEAD_DIM = 64
DIFF_HEADS = 4
DIFF_VDIM = 2 * HEAD_DIM
DIL_HEADS = 8
DIL_CONFIGS = ((128, 1), (512, 4), (2048, 16))
DIFF_WIDTH = DIFF_HEADS * DIFF_VDIM
DIL_WIDTH = DIL_HEADS * HEAD_DIM
ROPE_THETA = 500000.0
ROPE_DIM = HEAD_DIM // 4
EPS = 1e-5

LANES = 128
ROW_TILE = 512
FF_CHUNK = 256
PROJ_CHUNK = 512
Q_TILE = 256
KEY_CHUNK = 128
VMEM_LIMIT = 56 * 1024 * 1024

F32 = jnp.float32
BF16 = jnp.bfloat16


def _resident(shape):
    return pl.BlockSpec(shape, lambda *_: (0,) * len(shape), pipeline_mode=pl.Buffered(1))


def _rmsnorm(x, g):
    return x * lax.rsqrt(jnp.mean(x * x, axis=-1, keepdims=True) + EPS) * g


def _swiglu_half_step(x, g_ref, wg_ref, wu_ref, wd_ref, act_ref):
    h = _rmsnorm(x, g_ref[...]).astype(BF16)
    d_ff = wg_ref.shape[1]
    for c in range(d_ff // FF_CHUNK):
        cols = slice(c * FF_CHUNK, (c + 1) * FF_CHUNK)
        gate = jnp.dot(h, wg_ref[:, cols], preferred_element_type=F32)
        up = jnp.dot(h, wu_ref[:, cols], preferred_element_type=F32)
        act_ref[:, cols] = (gate * jax.nn.sigmoid(gate) * up).astype(BF16)
    y = jnp.dot(act_ref[...], wd_ref[...], preferred_element_type=F32)
    return x + 0.5 * y


def _ffn1_kernel(x_ref, g_ref, wg_ref, wu_ref, wd_ref, o_ref, act_ref):
    o_ref[...] = _swiglu_half_step(x_ref[...], g_ref, wg_ref, wu_ref, wd_ref, act_ref)


def _out_ffn2_kernel(x_ref, a_ref, b_ref, wo_ref, g_ref, wg_ref, wu_ref, wd_ref, gf_ref,
                     o_ref, act_ref, *, final_norm):
    wa = wo_ref[0:DIFF_WIDTH, :]
    wb = wo_ref[DIFF_WIDTH:DIFF_WIDTH + DIL_WIDTH, :]
    x = (x_ref[...]
         + jnp.dot(a_ref[...], wa, preferred_element_type=F32)
         + jnp.dot(b_ref[...], wb, preferred_element_type=F32))
    y = _swiglu_half_step(x, g_ref, wg_ref, wu_ref, wd_ref, act_ref)
    if final_norm:
        y = _rmsnorm(y, gf_ref[...])
    o_ref[...] = y


def _in_proj_kernel(x_ref, pos_ref, g_ref, w_ref, inv_ref, lo_ref, hi_ref, o_ref, *, rope_scale):
    h = _rmsnorm(x_ref[...], g_ref[...]).astype(BF16)
    ang = pos_ref[...].astype(F32) * inv_ref[...]
    cos = jnp.cos(ang)
    sin = jnp.sin(ang)
    sin_lo = sin * lo_ref[...]
    sin_hi = sin * hi_ref[...]
    half = ROPE_DIM // 2
    for c, scale in enumerate(rope_scale):
        cols = slice(c * PROJ_CHUNK, (c + 1) * PROJ_CHUNK)
        p = jnp.dot(h, w_ref[:, cols], preferred_element_type=F32)
        if scale is None:
            o_ref[:, cols] = p.astype(BF16)
            continue
        for s in range(PROJ_CHUNK // LANES):
            xs = p[:, s * LANES:(s + 1) * LANES]
            rot = (xs * cos
                   + pltpu.roll(xs, LANES - half, axis=1) * sin_lo
                   + pltpu.roll(xs, half, axis=1) * sin_hi)
            lo = c * PROJ_CHUNK + s * LANES
            o_ref[:, lo:lo + LANES] = (rot * scale).astype(BF16)


def _attention_maps(q_ref, k_ref, v_ref, bias_ref, o_ref, qt_scr, vt_scr, finalize):
    seq = q_ref.shape[0]
    width = bias_ref.shape[0]
    qt_scr[...] = q_ref[...].astype(F32).T.astype(BF16)
    vt_scr[...] = v_ref[...].astype(F32).T.astype(BF16)
    lo_rows = lax.broadcasted_iota(jnp.int32, (LANES, Q_TILE), 0) < HEAD_DIM
    for t in range(seq // Q_TILE):
        queries = slice(t * Q_TILE, (t + 1) * Q_TILE)
        qt = qt_scr[:, queries]
        outs = []
        for q_half in (jnp.where(lo_rows, qt, 0), jnp.where(lo_rows, 0, qt)):
            m = l = acc = None
            for j in range((t + 1) * Q_TILE // KEY_CHUNK):
                keys = slice(j * KEY_CHUNK, (j + 1) * KEY_CHUNK)
                s = jnp.dot(k_ref[keys, :], q_half, preferred_element_type=F32)
                first = width - Q_TILE + j * KEY_CHUNK - t * Q_TILE
                if first >= 0:
                    s = s + bias_ref[first:first + KEY_CHUNK, :]
                m_chunk = jnp.max(s, axis=0, keepdims=True)
                m_new = m_chunk if m is None else jnp.maximum(m, m_chunk)
                p = jnp.exp2(s - m_new)
                p_sum = jnp.sum(p, axis=0, keepdims=True)
                pv = jnp.dot(vt_scr[:, keys], p.astype(BF16), preferred_element_type=F32)
                if m is None:
                    l, acc = p_sum, pv
                else:
                    alpha = jnp.exp2(m - m_new)
                    l = alpha * l + p_sum
                    acc = alpha * acc + pv
                m = m_new
            outs.append(acc * (1.0 / l))
        o_ref[queries, :] = finalize(*outs).astype(BF16)


def _diff_attn_kernel(q_ref, k_ref, v_ref, bias_ref, lq1_ref, lk1_ref, lq2_ref, lk2_ref, gain_ref,
                      o_ref, qt_scr, vt_scr, *, lambda_init):
    lam = (jnp.exp(jnp.sum(lq1_ref[...] * lk1_ref[...], axis=-1, keepdims=True))
           - jnp.exp(jnp.sum(lq2_ref[...] * lk2_ref[...], axis=-1, keepdims=True))
           + lambda_init)

    def finalize(lo, hi):
        return _rmsnorm((lo - lam * hi).T, gain_ref[...]) * (1.0 - lambda_init)

    _attention_maps(q_ref, k_ref, v_ref, bias_ref, o_ref, qt_scr, vt_scr, finalize)


def _dil_attn_kernel(q_ref, k_ref, v_ref, bias_ref, o_ref, qt_scr, vt_scr):
    def finalize(lo, hi):
        return jnp.concatenate([lo[0:HEAD_DIM], hi[HEAD_DIM:LANES]], axis=0).T

    _attention_maps(q_ref, k_ref, v_ref, bias_ref, o_ref, qt_scr, vt_scr, finalize)


def _bias_table(width, branches):
    c = np.arange(width)[:, None]
    r = np.arange(Q_TILE)[None, :]
    d = r - c + width - Q_TILE
    count = np.zeros(d.shape, np.float64)
    for span, stride in branches:
        count += (d >= 0) & (d <= span) & (d % stride == 0)
    with np.errstate(divide="ignore"):
        return jnp.asarray(np.log2(count), dtype=F32)


def _rope_lane_tables():
    half = ROPE_DIM // 2
    inv = jnp.exp(-math.log(ROPE_THETA) * jnp.arange(half, dtype=F32) * 2.0 / ROPE_DIM)
    d = np.arange(LANES) % HEAD_DIM
    inv_lane = jnp.where(d < ROPE_DIM, jnp.tile(inv, LANES // half), 0.0)
    lo = np.where(d < half, -1.0, 0.0)
    hi = np.where((d >= half) & (d < ROPE_DIM), 1.0, 0.0)
    as_row = lambda t: jnp.asarray(t, F32).reshape(1, LANES)
    return as_row(inv_lane), as_row(lo), as_row(hi)


def _row_spec(cols):
    return pl.BlockSpec((ROW_TILE, cols), lambda i: (i, 0))


def _params(*semantics):
    return pltpu.CompilerParams(dimension_semantics=semantics, vmem_limit_bytes=VMEM_LIMIT)


def _ffn1(x, g, wg, wu, wd):
    t, d = x.shape
    f = wg.shape[1]
    return pl.pallas_call(
        _ffn1_kernel,
        grid=(t // ROW_TILE,),
        in_specs=[_row_spec(d), _resident((1, d)), _resident((d, f)), _resident((d, f)),
                  _resident((f, d))],
        out_specs=_row_spec(d),
        out_shape=jax.ShapeDtypeStruct((t, d), F32),
        scratch_shapes=[pltpu.VMEM((ROW_TILE, f), BF16)],
        compiler_params=_params("parallel"),
        name="ffn1",
    )(x, g, wg, wu, wd)


def _in_proj(x, pos, g, w, rope_tables, rope_scale):
    t, d = x.shape
    n = w.shape[1]
    return pl.pallas_call(
        functools.partial(_in_proj_kernel, rope_scale=rope_scale),
        grid=(t // ROW_TILE,),
        in_specs=[_row_spec(d), _row_spec(1), _resident((1, d)), _resident((d, n)),
                  _resident((1, LANES)), _resident((1, LANES)), _resident((1, LANES))],
        out_specs=_row_spec(n),
        out_shape=jax.ShapeDtypeStruct((t, n), BF16),
        compiler_params=_params("parallel"),
        name="in_proj",
    )(x, pos, g, w, *rope_tables)


def _attention(body, name, proj, bias, extra, *, q_col, k_col, v_col, groups):
    b, s, _ = proj.shape
    col_spec = lambda first: pl.BlockSpec((None, s, LANES), lambda bi, gi: (bi, 0, first + gi))
    return pl.pallas_call(
        body,
        grid=(b, groups),
        in_specs=[col_spec(q_col), col_spec(k_col), col_spec(v_col), _resident(bias.shape)]
                 + [_resident(e.shape) for e in extra],
        out_specs=pl.BlockSpec((None, s, LANES), lambda bi, gi: (bi, 0, gi)),
        out_shape=jax.ShapeDtypeStruct((b, s, groups * LANES), BF16),
        scratch_shapes=[pltpu.VMEM((LANES, s), BF16)] * 2,
        compiler_params=_params("parallel", "parallel"),
        name=name,
    )(proj, proj, proj, bias, *extra)


def _out_ffn2(x, a, b, wo, g, wg, wu, wd, gf, final_norm):
    t, d = x.shape
    f = wg.shape[1]
    return pl.pallas_call(
        functools.partial(_out_ffn2_kernel, final_norm=final_norm),
        grid=(t // ROW_TILE,),
        in_specs=[_row_spec(d), _row_spec(a.shape[1]), _row_spec(b.shape[1]),
                  _resident(wo.shape), _resident((1, d)), _resident((d, f)), _resident((d, f)),
                  _resident((f, d)), _resident((1, d))],
        out_specs=_row_spec(d),
        out_shape=jax.ShapeDtypeStruct((t, d), F32),
        scratch_shapes=[pltpu.VMEM((ROW_TILE, f), BF16)],
        compiler_params=_params("parallel"),
        name="out_ffn2",
    )(x, a, b, wo, g, wg, wu, wd, gf)


def kernel(x, positions, ffn1_norm, ffn1_gate, ffn1_up, ffn1_down, mix_norm, w_in,
           lambda_q1, lambda_k1, lambda_q2, lambda_k2, subln_gain, w_out,
           ffn2_norm, ffn2_gate, ffn2_up, ffn2_down, final_norm):
    bsz, seq, d = x.shape
    depth = ffn1_norm.shape[0]
    t = bsz * seq
    assert t % ROW_TILE == 0 and seq % Q_TILE == 0 and Q_TILE % KEY_CHUNK == 0
    assert w_in.shape[2] == 3 * (DIFF_WIDTH + DIL_WIDTH) and DIFF_WIDTH % PROJ_CHUNK == 0

    row = lambda v: v.astype(F32).reshape(1, -1)
    q_scale = HEAD_DIM ** -0.5 * math.log2(math.e)
    rope_scale = (q_scale, 1.0, None, q_scale, 1.0, None)
    rope_scale = tuple(s for s in rope_scale for _ in range(DIFF_WIDTH // PROJ_CHUNK))
    groups = DIFF_WIDTH // LANES
    rope_tables = _rope_lane_tables()
    causal_bias = _bias_table(Q_TILE, ((seq, 1),))
    dilated_bias = _bias_table(seq, DIL_CONFIGS)
    pos = positions.reshape(t, 1)

    h = x.reshape(t, d)
    for l in range(depth):
        lambda_init = 0.8 - 0.6 * math.exp(-0.3 * l)
        h = _ffn1(h, row(ffn1_norm[l]), ffn1_gate[l].astype(BF16), ffn1_up[l].astype(BF16),
                  ffn1_down[l].astype(BF16))
        proj = _in_proj(h, pos, row(mix_norm[l]), w_in[l].astype(BF16), rope_tables, rope_scale)
        proj = proj.reshape(bsz, seq, -1)
        lams = (row(lambda_q1[l]), row(lambda_k1[l]), row(lambda_q2[l]), row(lambda_k2[l]))
        a = _attention(functools.partial(_diff_attn_kernel, lambda_init=lambda_init), "diff_attn",
                       proj, causal_bias, (*lams, row(subln_gain[l])),
                       q_col=0, k_col=groups, v_col=2 * groups, groups=groups)
        b = _attention(_dil_attn_kernel, "dil_attn", proj, dilated_bias, (),
                       q_col=3 * groups, k_col=4 * groups, v_col=5 * groups, groups=groups)
        h = _out_ffn2(h, a.reshape(t, -1), b.reshape(t, -1), w_out[l].astype(BF16),
                      row(ffn2_norm[l]), ffn2_gate[l].astype(BF16), ffn2_up[l].astype(BF16),
                      ffn2_down[l].astype(BF16), row(final_norm), final_norm=(l == depth - 1))
    return h.reshape(bsz, seq, d)
```

```python
import functools
import math

import numpy as np
import jax
import jax.numpy as jnp
from jax import lax
from jax.experimental import pallas as pl
from jax.experimental.pallas import tpu as pltpu

HEAD_DIM = 64
DIFF_HEADS = 4
DIFF_VDIM = 2 * HEAD_DIM
DIL_HEADS = 8
DIL_CONFIGS = ((128, 1), (512, 4), (2048, 16))
DIFF_WIDTH = DIFF_HEADS * DIFF_VDIM
DIL_WIDTH = DIL_HEADS * HEAD_DIM
ROPE_THETA = 500000.0
ROPE_DIM = HEAD_DIM // 4
EPS = 1e-5

LANES = 128
ROW_TILE = 512
FF_CHUNK = 256
PROJ_CHUNK = 512
Q_TILE = 256
KEY_CHUNK = 128
VMEM_LIMIT = 56 * 1024 * 1024

F32 = jnp.float32
BF16 = jnp.bfloat16


def _resident(shape):
    return pl.BlockSpec(shape, lambda *_: (0,) * len(shape), pipeline_mode=pl.Buffered(1))


def _rmsnorm(x, g):
    return x * lax.rsqrt(jnp.mean(x * x, axis=-1, keepdims=True) + EPS) * g


def _swiglu_half_step(x, g_ref, wg_ref, wu_ref, wd_ref, act_ref):
    h = _rmsnorm(x, g_ref[...]).astype(BF16)
    d_ff = wg_ref.shape[1]
    for c in range(d_ff // FF_CHUNK):
        cols = slice(c * FF_CHUNK, (c + 1) * FF_CHUNK)
        gate = jnp.dot(h, wg_ref[:, cols], preferred_element_type=F32)
        up = jnp.dot(h, wu_ref[:, cols], preferred_element_type=F32)
        act_ref[:, cols] = (gate * jax.nn.sigmoid(gate) * up).astype(BF16)
    y = jnp.dot(act_ref[...], wd_ref[...], preferred_element_type=F32)
    return x + 0.5 * y


def _ffn1_kernel(x_ref, g_ref, wg_ref, wu_ref, wd_ref, o_ref, act_ref):
    o_ref[...] = _swiglu_half_step(x_ref[...], g_ref, wg_ref, wu_ref, wd_ref, act_ref)


def _out_ffn2_kernel(x_ref, a_ref, b_ref, wo_ref, g_ref, wg_ref, wu_ref, wd_ref, gf_ref,
                     o_ref, act_ref, *, final_norm):
    wa = wo_ref[0:DIFF_WIDTH, :]
    wb = wo_ref[DIFF_WIDTH:DIFF_WIDTH + DIL_WIDTH, :]
    x = (x_ref[...]
         + jnp.dot(a_ref[...], wa, preferred_element_type=F32)
         + jnp.dot(b_ref[...], wb, preferred_element_type=F32))
    y = _swiglu_half_step(x, g_ref, wg_ref, wu_ref, wd_ref, act_ref)
    if final_norm:
        y = _rmsnorm(y, gf_ref[...])
    o_ref[...] = y


def _rope_tables(pos_ref, inv_ref):
    half = ROPE_DIM // 2
    pos = pos_ref[...].astype(F32)
    pos_row = jnp.concatenate([pos[a:a + 1, :] for a in range(pos.shape[0])], axis=1)
    ang = inv_ref[...] * pos_row
    cos, sin = jnp.cos(ang), jnp.sin(ang)
    rows = pos_row.shape[1]
    one = jnp.ones((HEAD_DIM - ROPE_DIM, rows), F32)
    zero = jnp.zeros((HEAD_DIM - ROPE_DIM, rows), F32)
    zero_half = jnp.zeros((half, rows), F32)
    heads = LANES // HEAD_DIM
    table = lambda head_rows: jnp.concatenate(head_rows * heads, axis=0).T
    return (table([cos, cos, one]),
            table([-sin, zero_half, zero]),
            table([zero_half, sin, zero]))


def _in_proj_kernel(x_ref, pos_ref, g_ref, w_ref, inv_ref, o_ref, *, rope_scale):
    h = _rmsnorm(x_ref[...], g_ref[...]).astype(BF16)
    tables = {1.0: _rope_tables(pos_ref, inv_ref)}
    half = ROPE_DIM // 2
    for c, scale in enumerate(rope_scale):
        cols = slice(c * PROJ_CHUNK, (c + 1) * PROJ_CHUNK)
        p = jnp.dot(h, w_ref[:, cols], preferred_element_type=F32)
        if scale is None:
            o_ref[:, cols] = p.astype(BF16)
            continue
        if scale not in tables:
            tables[scale] = tuple(t * scale for t in tables[1.0])
        cos, sin_lo, sin_hi = tables[scale]
        for s in range(PROJ_CHUNK // LANES):
            xs = p[:, s * LANES:(s + 1) * LANES]
            rot = (xs * cos
                   + pltpu.roll(xs, LANES - half, axis=1) * sin_lo
                   + pltpu.roll(xs, half, axis=1) * sin_hi)
            lo = c * PROJ_CHUNK + s * LANES
            o_ref[:, lo:lo + LANES] = rot.astype(BF16)


def _attention_maps(q_ref, k_ref, v_ref, bias_ref, o_ref, qt_scr, vt_scr, finalize):
    seq = q_ref.shape[0]
    width = bias_ref.shape[0]
    qt_scr[...] = q_ref[...].astype(F32).T.astype(BF16)
    vt_scr[...] = v_ref[...].astype(F32).T.astype(BF16)
    lo_rows = lax.broadcasted_iota(jnp.int32, (LANES, Q_TILE), 0) < HEAD_DIM
    for t in range(seq // Q_TILE):
        queries = slice(t * Q_TILE, (t + 1) * Q_TILE)
        qt = qt_scr[:, queries]
        outs = []
        for q_half in (jnp.where(lo_rows, qt, 0), jnp.where(lo_rows, 0, qt)):
            m = l = acc = None
            for j in range((t + 1) * Q_TILE // KEY_CHUNK):
                keys = slice(j * KEY_CHUNK, (j + 1) * KEY_CHUNK)
                s = jnp.dot(k_ref[keys, :], q_half, preferred_element_type=F32)
                first = width - Q_TILE + j * KEY_CHUNK - t * Q_TILE
                if first >= 0:
                    s = s + bias_ref[first:first + KEY_CHUNK, :]
                m_chunk = jnp.max(s, axis=0, keepdims=True)
                m_new = m_chunk if m is None else jnp.maximum(m, m_chunk)
                p = jnp.exp2(s - m_new)
                p_sum = jnp.sum(p, axis=0, keepdims=True)
                pv = jnp.dot(vt_scr[:, keys], p.astype(BF16), preferred_element_type=F32)
                if m is None:
                    l, acc = p_sum, pv
                else:
                    alpha = jnp.exp2(m - m_new)
                    l = alpha * l + p_sum
                    acc = alpha * acc + pv
                m = m_new
            outs.append(acc * (1.0 / l))
        o_ref[queries, :] = finalize(*outs).astype(BF16)


def _diff_attn_kernel(q_ref, k_ref, v_ref, bias_ref, lq1_ref, lk1_ref, lq2_ref, lk2_ref, gain_ref,
                      o_ref, qt_scr, vt_scr, *, lambda_init):
    lam = (jnp.exp(jnp.sum(lq1_ref[...] * lk1_ref[...], axis=-1, keepdims=True))
           - jnp.exp(jnp.sum(lq2_ref[...] * lk2_ref[...], axis=-1, keepdims=True))
           + lambda_init)

    def finalize(lo, hi):
        return _rmsnorm((lo - lam * hi).T, gain_ref[...]) * (1.0 - lambda_init)

    _attention_maps(q_ref, k_ref, v_ref, bias_ref, o_ref, qt_scr, vt_scr, finalize)


def _dil_attn_kernel(q_ref, k_ref, v_ref, bias_ref, o_ref, qt_scr, vt_scr):
    def finalize(lo, hi):
        return jnp.concatenate([lo[0:HEAD_DIM], hi[HEAD_DIM:LANES]], axis=0).T

    _attention_maps(q_ref, k_ref, v_ref, bias_ref, o_ref, qt_scr, vt_scr, finalize)


def _bias_table(width, branches):
    c = np.arange(width)[:, None]
    r = np.arange(Q_TILE)[None, :]
    d = r - c + width - Q_TILE
    count = np.zeros(d.shape, np.float64)
    for span, stride in branches:
        count += (d >= 0) & (d <= span) & (d % stride == 0)
    with np.errstate(divide="ignore"):
        return jnp.asarray(np.log2(count), dtype=F32)


def _row_spec(cols):
    return pl.BlockSpec((ROW_TILE, cols), lambda i: (i, 0))


def _params(*semantics):
    return pltpu.CompilerParams(dimension_semantics=semantics, vmem_limit_bytes=VMEM_LIMIT)


def _ffn1(x, g, wg, wu, wd):
    t, d = x.shape
    f = wg.shape[1]
    return pl.pallas_call(
        _ffn1_kernel,
        grid=(t // ROW_TILE,),
        in_specs=[_row_spec(d), _resident((1, d)), _resident((d, f)), _resident((d, f)),
                  _resident((f, d))],
        out_specs=_row_spec(d),
        out_shape=jax.ShapeDtypeStruct((t, d), F32),
        scratch_shapes=[pltpu.VMEM((ROW_TILE, f), BF16)],
        compiler_params=_params("parallel"),
        name="ffn1",
    )(x, g, wg, wu, wd)


def _in_proj(x, pos, g, w, rope_scale):
    t, d = x.shape
    n = w.shape[1]
    half = ROPE_DIM // 2
    inv = jnp.exp(-math.log(ROPE_THETA) * jnp.arange(half, dtype=F32) * 2.0 / ROPE_DIM)
    pos_tiles = pos.reshape(t // ROW_TILE, ROW_TILE // LANES, LANES)
    return pl.pallas_call(
        functools.partial(_in_proj_kernel, rope_scale=rope_scale),
        grid=(t // ROW_TILE,),
        in_specs=[_row_spec(d),
                  pl.BlockSpec((None, ROW_TILE // LANES, LANES), lambda i: (i, 0, 0)),
                  _resident((1, d)), _resident((d, n)), _resident((half, 1))],
        out_specs=_row_spec(n),
        out_shape=jax.ShapeDtypeStruct((t, n), BF16),
        compiler_params=_params("parallel"),
        name="in_proj",
    )(x, pos_tiles, g, w, inv.reshape(half, 1))


def _attention(body, name, proj, bias, extra, *, q_col, k_col, v_col, groups):
    b, s, _ = proj.shape
    col_spec = lambda first: pl.BlockSpec((None, s, LANES), lambda bi, gi: (bi, 0, first + gi))
    return pl.pallas_call(
        body,
        grid=(b, groups),
        in_specs=[col_spec(q_col), col_spec(k_col), col_spec(v_col), _resident(bias.shape)]
                 + [_resident(e.shape) for e in extra],
        out_specs=pl.BlockSpec((None, s, LANES), lambda bi, gi: (bi, 0, gi)),
        out_shape=jax.ShapeDtypeStruct((b, s, groups * LANES), BF16),
        scratch_shapes=[pltpu.VMEM((LANES, s), BF16)] * 2,
        compiler_params=_params("parallel", "parallel"),
        name=name,
    )(proj, proj, proj, bias, *extra)


def _out_ffn2(x, a, b, wo, g, wg, wu, wd, gf, final_norm):
    t, d = x.shape
    f = wg.shape[1]
    return pl.pallas_call(
        functools.partial(_out_ffn2_kernel, final_norm=final_norm),
        grid=(t // ROW_TILE,),
        in_specs=[_row_spec(d), _row_spec(a.shape[1]), _row_spec(b.shape[1]),
                  _resident(wo.shape), _resident((1, d)), _resident((d, f)), _resident((d, f)),
                  _resident((f, d)), _resident((1, d))],
        out_specs=_row_spec(d),
        out_shape=jax.ShapeDtypeStruct((t, d), F32),
        scratch_shapes=[pltpu.VMEM((ROW_TILE, f), BF16)],
        compiler_params=_params("parallel"),
        name="out_ffn2",
    )(x, a, b, wo, g, wg, wu, wd, gf)


def kernel(x, positions, ffn1_norm, ffn1_gate, ffn1_up, ffn1_down, mix_norm, w_in,
           lambda_q1, lambda_k1, lambda_q2, lambda_k2, subln_gain, w_out,
           ffn2_norm, ffn2_gate, ffn2_up, ffn2_down, final_norm):
    bsz, seq, d = x.shape
    depth = ffn1_norm.shape[0]
    t = bsz * seq
    assert t % ROW_TILE == 0 and seq % Q_TILE == 0 and Q_TILE % KEY_CHUNK == 0
    assert w_in.shape[2] == 3 * (DIFF_WIDTH + DIL_WIDTH) and DIFF_WIDTH % PROJ_CHUNK == 0

    row = lambda v: v.astype(F32).reshape(1, -1)
    q_scale = HEAD_DIM ** -0.5 * math.log2(math.e)
    rope_scale = (q_scale, 1.0, None, q_scale, 1.0, None)
    rope_scale = tuple(s for s in rope_scale for _ in range(DIFF_WIDTH // PROJ_CHUNK))
    groups = DIFF_WIDTH // LANES
    causal_bias = _bias_table(Q_TILE, ((seq, 1),))
    dilated_bias = _bias_table(seq, DIL_CONFIGS)
    pos = positions.reshape(t)

    h = x.reshape(t, d)
    for l in range(depth):
        lambda_init = 0.8 - 0.6 * math.exp(-0.3 * l)
        h = _ffn1(h, row(ffn1_norm[l]), ffn1_gate[l].astype(BF16), ffn1_up[l].astype(BF16),
                  ffn1_down[l].astype(BF16))
        proj = _in_proj(h, pos, row(mix_norm[l]), w_in[l].astype(BF16), rope_scale)
        proj = proj.reshape(bsz, seq, -1)
        lams = (row(lambda_q1[l]), row(lambda_k1[l]), row(lambda_q2[l]), row(lambda_k2[l]))
        a = _attention(functools.partial(_diff_attn_kernel, lambda_init=lambda_init), "diff_attn",
                       proj, causal_bias, (*lams, row(subln_gain[l])),
                       q_col=0, k_col=groups, v_col=2 * groups, groups=groups)
        b = _attention(_dil_attn_kernel, "dil_attn", proj, dilated_bias, (),
                       q_col=3 * groups, k_col=4 * groups, v_col=5 * groups, groups=groups)
        h = _out_ffn2(h, a.reshape(t, -1), b.reshape(t, -1), w_out[l].astype(BF16),
                      row(ffn2_norm[l]), ffn2_gate[l].astype(BF16), ffn2_up[l].astype(BF16),
                      ffn2_down[l].astype(BF16), row(final_norm), final_norm=(l == depth - 1))
    return h.reshape(bsz, seq, d)
```

```python
import functools
import math

import numpy as np
import jax
import jax.numpy as jnp
from jax import lax
from jax.experimental import pallas as pl
from jax.experimental.pallas import tpu as pltpu

HEAD_DIM = 64
DIFF_HEADS = 4
DIFF_VDIM = 2 * HEAD_DIM
DIL_HEADS = 8
DIL_CONFIGS = ((128, 1), (512, 4), (2048, 16))
DIFF_WIDTH = DIFF_HEADS * DIFF_VDIM
DIL_WIDTH = DIL_HEADS * HEAD_DIM
ROPE_THETA = 500000.0
ROPE_DIM = HEAD_DIM // 4
EPS = 1e-5

LANES = 128
ROW_TILE = 512
FFN_SUBTILES = 2
FF_CHUNK = 256
PROJ_CHUNK = 512
Q_TILE = 256
KEY_CHUNK = 128
VMEM_LIMIT = 56 * 1024 * 1024

F32 = jnp.float32
BF16 = jnp.bfloat16


def _resident(shape):
    return pl.BlockSpec(shape, lambda *_: (0,) * len(shape), pipeline_mode=pl.Buffered(1))


def _rmsnorm(x, g):
    return x * lax.rsqrt(jnp.mean(x * x, axis=-1, keepdims=True) + EPS) * g


def _swiglu_half_step(x, g_ref, wg_ref, wu_ref, wd_ref, act_ref):
    h = _rmsnorm(x, g_ref[...]).astype(BF16)
    d_ff = wg_ref.shape[1]
    for c in range(d_ff // FF_CHUNK):
        cols = slice(c * FF_CHUNK, (c + 1) * FF_CHUNK)
        gate = jnp.dot(h, wg_ref[:, cols], preferred_element_type=F32)
        up = jnp.dot(h, wu_ref[:, cols], preferred_element_type=F32)
        act_ref[:, cols] = (gate * jax.nn.sigmoid(gate) * up).astype(BF16)
    y = jnp.dot(act_ref[...], wd_ref[...], preferred_element_type=F32)
    return x + 0.5 * y


def _subtiles(act_refs):
    return [(slice(i * ROW_TILE, (i + 1) * ROW_TILE), act) for i, act in enumerate(act_refs)]


def _ffn1_kernel(x_ref, g_ref, wg_ref, wu_ref, wd_ref, o_ref, *act_refs):
    for rows, act_ref in _subtiles(act_refs):
        o_ref[rows, :] = _swiglu_half_step(x_ref[rows, :], g_ref, wg_ref, wu_ref, wd_ref, act_ref)


def _out_ffn2_kernel(x_ref, a_ref, b_ref, wo_ref, g_ref, wg_ref, wu_ref, wd_ref, gf_ref,
                     o_ref, *act_refs, final_norm):
    wa = wo_ref[0:DIFF_WIDTH, :]
    wb = wo_ref[DIFF_WIDTH:DIFF_WIDTH + DIL_WIDTH, :]
    for rows, act_ref in _subtiles(act_refs):
        x = (x_ref[rows, :]
             + jnp.dot(a_ref[rows, :], wa, preferred_element_type=F32)
             + jnp.dot(b_ref[rows, :], wb, preferred_element_type=F32))
        y = _swiglu_half_step(x, g_ref, wg_ref, wu_ref, wd_ref, act_ref)
        if final_norm:
            y = _rmsnorm(y, gf_ref[...])
        o_ref[rows, :] = y


def _rope_tables(pos_ref, inv_ref):
    half = ROPE_DIM // 2
    pos = pos_ref[...].astype(F32)
    pos_row = jnp.concatenate([pos[a:a + 1, :] for a in range(pos.shape[0])], axis=1)
    ang = inv_ref[...] * pos_row
    cos, sin = jnp.cos(ang), jnp.sin(ang)
    rows = pos_row.shape[1]
    one = jnp.ones((HEAD_DIM - ROPE_DIM, rows), F32)
    zero = jnp.zeros((HEAD_DIM - ROPE_DIM, rows), F32)
    zero_half = jnp.zeros((half, rows), F32)
    heads = LANES // HEAD_DIM
    table = lambda head_rows: jnp.concatenate(head_rows * heads, axis=0).T
    return (table([cos, cos, one]),
            table([-sin, zero_half, zero]),
            table([zero_half, sin, zero]))


def _in_proj_kernel(x_ref, pos_ref, g_ref, w_ref, inv_ref, o_ref, *, rope_scale):
    h = _rmsnorm(x_ref[...], g_ref[...]).astype(BF16)
    tables = {1.0: _rope_tables(pos_ref, inv_ref)}
    half = ROPE_DIM // 2
    for c, scale in enumerate(rope_scale):
        cols = slice(c * PROJ_CHUNK, (c + 1) * PROJ_CHUNK)
        p = jnp.dot(h, w_ref[:, cols], preferred_element_type=F32)
        if scale is None:
            o_ref[:, cols] = p.astype(BF16)
            continue
        if scale not in tables:
            tables[scale] = tuple(t * scale for t in tables[1.0])
        cos, sin_lo, sin_hi = tables[scale]
        for s in range(PROJ_CHUNK // LANES):
            xs = p[:, s * LANES:(s + 1) * LANES]
            rot = (xs * cos
                   + pltpu.roll(xs, LANES - half, axis=1) * sin_lo
                   + pltpu.roll(xs, half, axis=1) * sin_hi)
            lo = c * PROJ_CHUNK + s * LANES
            o_ref[:, lo:lo + LANES] = rot.astype(BF16)


def _attention_maps(q_ref, k_ref, v_ref, bias_ref, o_ref, qt_scr, vt_scr, finalize):
    seq = q_ref.shape[0]
    width = bias_ref.shape[0]
    qt_scr[...] = q_ref[...].astype(F32).T.astype(BF16)
    vt_scr[...] = v_ref[...].astype(F32).T.astype(BF16)
    lo_rows = lax.broadcasted_iota(jnp.int32, (LANES, Q_TILE), 0) < HEAD_DIM
    for t in range(seq // Q_TILE):
        queries = slice(t * Q_TILE, (t + 1) * Q_TILE)
        qt = qt_scr[:, queries]
        outs = []
        for q_half in (jnp.where(lo_rows, qt, 0), jnp.where(lo_rows, 0, qt)):
            m = l = acc = None
            for j in range((t + 1) * Q_TILE // KEY_CHUNK):
                keys = slice(j * KEY_CHUNK, (j + 1) * KEY_CHUNK)
                s = jnp.dot(k_ref[keys, :], q_half, preferred_element_type=F32)
                first = width - Q_TILE + j * KEY_CHUNK - t * Q_TILE
                if first >= 0:
                    s = s + bias_ref[first:first + KEY_CHUNK, :]
                m_chunk = jnp.max(s, axis=0, keepdims=True)
                m_new = m_chunk if m is None else jnp.maximum(m, m_chunk)
                p = jnp.exp2(s - m_new)
                p_sum = jnp.sum(p, axis=0, keepdims=True)
                pv = jnp.dot(vt_scr[:, keys], p.astype(BF16), preferred_element_type=F32)
                if m is None:
                    l, acc = p_sum, pv
                else:
                    alpha = jnp.exp2(m - m_new)
                    l = alpha * l + p_sum
                    acc = alpha * acc + pv
                m = m_new
            outs.append(acc * (1.0 / l))
        o_ref[queries, :] = finalize(*outs).astype(BF16)


def _diff_attn_kernel(q_ref, k_ref, v_ref, bias_ref, lq1_ref, lk1_ref, lq2_ref, lk2_ref, gain_ref,
                      o_ref, qt_scr, vt_scr, *, lambda_init):
    lam = (jnp.exp(jnp.sum(lq1_ref[...] * lk1_ref[...], axis=-1, keepdims=True))
           - jnp.exp(jnp.sum(lq2_ref[...] * lk2_ref[...], axis=-1, keepdims=True))
           + lambda_init)

    def finalize(lo, hi):
        return _rmsnorm((lo - lam * hi).T, gain_ref[...]) * (1.0 - lambda_init)

    _attention_maps(q_ref, k_ref, v_ref, bias_ref, o_ref, qt_scr, vt_scr, finalize)


def _dil_attn_kernel(q_ref, k_ref, v_ref, bias_ref, o_ref, qt_scr, vt_scr):
    def finalize(lo, hi):
        return jnp.concatenate([lo[0:HEAD_DIM], hi[HEAD_DIM:LANES]], axis=0).T

    _attention_maps(q_ref, k_ref, v_ref, bias_ref, o_ref, qt_scr, vt_scr, finalize)


def _bias_table(width, branches):
    c = np.arange(width)[:, None]
    r = np.arange(Q_TILE)[None, :]
    d = r - c + width - Q_TILE
    count = np.zeros(d.shape, np.float64)
    for span, stride in branches:
        count += (d >= 0) & (d <= span) & (d % stride == 0)
    with np.errstate(divide="ignore"):
        return jnp.asarray(np.log2(count), dtype=F32)


def _row_spec(cols, tiles=1):
    return pl.BlockSpec((tiles * ROW_TILE, cols), lambda i: (i, 0))


def _params(*semantics):
    return pltpu.CompilerParams(dimension_semantics=semantics, vmem_limit_bytes=VMEM_LIMIT)


def _ffn1(x, g, wg, wu, wd):
    t, d = x.shape
    f = wg.shape[1]
    return pl.pallas_call(
        _ffn1_kernel,
        grid=(t // (FFN_SUBTILES * ROW_TILE),),
        in_specs=[_row_spec(d, FFN_SUBTILES), _resident((1, d)), _resident((d, f)),
                  _resident((d, f)), _resident((f, d))],
        out_specs=_row_spec(d, FFN_SUBTILES),
        out_shape=jax.ShapeDtypeStruct((t, d), F32),
        scratch_shapes=[pltpu.VMEM((ROW_TILE, f), BF16)] * FFN_SUBTILES,
        compiler_params=_params("parallel"),
        name="ffn1",
    )(x, g, wg, wu, wd)


def _in_proj(x, pos, g, w, rope_scale):
    t, d = x.shape
    n = w.shape[1]
    half = ROPE_DIM // 2
    inv = jnp.exp(-math.log(ROPE_THETA) * jnp.arange(half, dtype=F32) * 2.0 / ROPE_DIM)
    pos_tiles = pos.reshape(t // ROW_TILE, ROW_TILE // LANES, LANES)
    return pl.pallas_call(
        functools.partial(_in_proj_kernel, rope_scale=rope_scale),
        grid=(t // ROW_TILE,),
        in_specs=[_row_spec(d),
                  pl.BlockSpec((None, ROW_TILE // LANES, LANES), lambda i: (i, 0, 0)),
                  _resident((1, d)), _resident((d, n)), _resident((half, 1))],
        out_specs=_row_spec(n),
        out_shape=jax.ShapeDtypeStruct((t, n), BF16),
        compiler_params=_params("parallel"),
        name="in_proj",
    )(x, pos_tiles, g, w, inv.reshape(half, 1))


def _attention(body, name, proj, bias, extra, *, q_col, k_col, v_col, groups):
    b, s, _ = proj.shape
    col_spec = lambda first: pl.BlockSpec((None, s, LANES), lambda bi, gi: (bi, 0, first + gi))
    return pl.pallas_call(
        body,
        grid=(b, groups),
        in_specs=[col_spec(q_col), col_spec(k_col), col_spec(v_col), _resident(bias.shape)]
                 + [_resident(e.shape) for e in extra],
        out_specs=pl.BlockSpec((None, s, LANES), lambda bi, gi: (bi, 0, gi)),
        out_shape=jax.ShapeDtypeStruct((b, s, groups * LANES), BF16),
        scratch_shapes=[pltpu.VMEM((LANES, s), BF16)] * 2,
        compiler_params=_params("parallel", "parallel"),
        name=name,
    )(proj, proj, proj, bias, *extra)


def _out_ffn2(x, a, b, wo, g, wg, wu, wd, gf, final_norm):
    t, d = x.shape
    f = wg.shape[1]
    return pl.pallas_call(
        functools.partial(_out_ffn2_kernel, final_norm=final_norm),
        grid=(t // (FFN_SUBTILES * ROW_TILE),),
        in_specs=[_row_spec(d, FFN_SUBTILES), _row_spec(a.shape[1], FFN_SUBTILES),
                  _row_spec(b.shape[1], FFN_SUBTILES),
                  _resident(wo.shape), _resident((1, d)), _resident((d, f)), _resident((d, f)),
                  _resident((f, d)), _resident((1, d))],
        out_specs=_row_spec(d, FFN_SUBTILES),
        out_shape=jax.ShapeDtypeStruct((t, d), F32),
        scratch_shapes=[pltpu.VMEM((ROW_TILE, f), BF16)] * FFN_SUBTILES,
        compiler_params=_params("parallel"),
        name="out_ffn2",
    )(x, a, b, wo, g, wg, wu, wd, gf)


def kernel(x, positions, ffn1_norm, ffn1_gate, ffn1_up, ffn1_down, mix_norm, w_in,
           lambda_q1, lambda_k1, lambda_q2, lambda_k2, subln_gain, w_out,
           ffn2_norm, ffn2_gate, ffn2_up, ffn2_down, final_norm):
    bsz, seq, d = x.shape
    depth = ffn1_norm.shape[0]
    t = bsz * seq
    assert t % (FFN_SUBTILES * ROW_TILE) == 0 and seq % Q_TILE == 0 and Q_TILE % KEY_CHUNK == 0
    assert w_in.shape[2] == 3 * (DIFF_WIDTH + DIL_WIDTH) and DIFF_WIDTH % PROJ_CHUNK == 0

    row = lambda v: v.astype(F32).reshape(1, -1)
    q_scale = HEAD_DIM ** -0.5 * math.log2(math.e)
    rope_scale = (q_scale, 1.0, None, q_scale, 1.0, None)
    rope_scale = tuple(s for s in rope_scale for _ in range(DIFF_WIDTH // PROJ_CHUNK))
    groups = DIFF_WIDTH // LANES
    causal_bias = _bias_table(Q_TILE, ((seq, 1),))
    dilated_bias = _bias_table(seq, DIL_CONFIGS)
    pos = positions.reshape(t)

    h = x.reshape(t, d)
    for l in range(depth):
        lambda_init = 0.8 - 0.6 * math.exp(-0.3 * l)
        h = _ffn1(h, row(ffn1_norm[l]), ffn1_gate[l].astype(BF16), ffn1_up[l].astype(BF16),
                  ffn1_down[l].astype(BF16))
        proj = _in_proj(h, pos, row(mix_norm[l]), w_in[l].astype(BF16), rope_scale)
        proj = proj.reshape(bsz, seq, -1)
        lams = (row(lambda_q1[l]), row(lambda_k1[l]), row(lambda_q2[l]), row(lambda_k2[l]))
        a = _attention(functools.partial(_diff_attn_kernel, lambda_init=lambda_init), "diff_attn",
                       proj, causal_bias, (*lams, row(subln_gain[l])),
                       q_col=0, k_col=groups, v_col=2 * groups, groups=groups)
        b = _attention(_dil_attn_kernel, "dil_attn", proj, dilated_bias, (),
                       q_col=3 * groups, k_col=4 * groups, v_col=5 * groups, groups=groups)
        h = _out_ffn2(h, a.reshape(t, -1), b.reshape(t, -1), w_out[l].astype(BF16),
                      row(ffn2_norm[l]), ffn2_gate[l].astype(BF16), ffn2_up[l].astype(BF16),
                      ffn2_down[l].astype(BF16), row(final_norm), final_norm=(l == depth - 1))
    return h.reshape(bsz, seq, d)
```

```python
import functools
import math

import numpy as np
import jax
import jax.numpy as jnp
from jax import lax
from jax.experimental import pallas as pl
from jax.experimental.pallas import tpu as pltpu

HEAD_DIM = 64
DIFF_HEADS = 4
DIFF_VDIM = 2 * HEAD_DIM
DIL_HEADS = 8
DIL_CONFIGS = ((128, 1), (512, 4), (2048, 16))
DIFF_WIDTH = DIFF_HEADS * DIFF_VDIM
DIL_WIDTH = DIL_HEADS * HEAD_DIM
ROPE_THETA = 500000.0
ROPE_DIM = HEAD_DIM // 4
EPS = 1e-5

LANES = 128
ROW_TILE = 512
FF_CHUNK = 256
PROJ_CHUNK = 512
Q_TILE = 256
KEY_CHUNK = 128
VMEM_LIMIT = 60 * 1024 * 1024

F32 = jnp.float32
BF16 = jnp.bfloat16
MATMUL_DIMS = (((1,), (0,)), ((), ()))


def _matmul(a, b):
    return lax.dot_general(a, b, MATMUL_DIMS, preferred_element_type=F32)


def _resident(shape):
    return pl.BlockSpec(shape, lambda *_: (0,) * len(shape), pipeline_mode=pl.Buffered(1))


def _rmsnorm(x, g):
    return x * lax.rsqrt(jnp.mean(x * x, axis=-1, keepdims=True) + EPS) * g


def _swiglu_half_step(x, g_ref, wg_ref, wu_ref, wd_ref, act_ref):
    h = _rmsnorm(x, g_ref[...]).astype(BF16)
    d_ff = wg_ref.shape[1]
    for c in range(d_ff // FF_CHUNK):
        cols = slice(c * FF_CHUNK, (c + 1) * FF_CHUNK)
        gate = _matmul(h, wg_ref[:, cols])
        up = _matmul(h, wu_ref[:, cols])
        act_ref[:, cols] = (gate * jax.nn.sigmoid(gate) * up).astype(BF16)
    y = _matmul(act_ref[...], wd_ref[...])
    return x + 0.5 * y


def _ffn1_kernel(x_ref, g_ref, wg_ref, wu_ref, wd_ref, o_ref, act_ref):
    o_ref[...] = _swiglu_half_step(x_ref[...], g_ref, wg_ref, wu_ref, wd_ref, act_ref)


def _out_ffn2_kernel(x_ref, a_ref, b_ref, wo_ref, g_ref, wg_ref, wu_ref, wd_ref, gf_ref,
                     o_ref, act_ref, *, final_norm):
    wa = wo_ref[0:DIFF_WIDTH, :]
    wb = wo_ref[DIFF_WIDTH:DIFF_WIDTH + DIL_WIDTH, :]
    x = x_ref[...] + _matmul(a_ref[...], wa) + _matmul(b_ref[...], wb)
    y = _swiglu_half_step(x, g_ref, wg_ref, wu_ref, wd_ref, act_ref)
    if final_norm:
        y = _rmsnorm(y, gf_ref[...])
    o_ref[...] = y


def _rope_tables(pos_ref, inv_ref):
    half = ROPE_DIM // 2
    pos = pos_ref[...].astype(F32)
    pos_row = jnp.concatenate([pos[a:a + 1, :] for a in range(pos.shape[0])], axis=1)
    ang = inv_ref[...] * pos_row
    cos, sin = jnp.cos(ang), jnp.sin(ang)
    rows = pos_row.shape[1]
    one = jnp.ones((HEAD_DIM - ROPE_DIM, rows), F32)
    zero = jnp.zeros((HEAD_DIM - ROPE_DIM, rows), F32)
    zero_half = jnp.zeros((half, rows), F32)
    heads = LANES // HEAD_DIM
    table = lambda head_rows: jnp.concatenate(head_rows * heads, axis=0).T
    return (table([cos, cos, one]),
            table([-sin, zero_half, zero]),
            table([zero_half, sin, zero]))


def _in_proj_kernel(x_ref, pos_ref, g_ref, w_ref, inv_ref, o_ref, *, rope_scale):
    h = _rmsnorm(x_ref[...], g_ref[...]).astype(BF16)
    tables = {1.0: _rope_tables(pos_ref, inv_ref)}
    half = ROPE_DIM // 2
    for c, scale in enumerate(rope_scale):
        cols = slice(c * PROJ_CHUNK, (c + 1) * PROJ_CHUNK)
        p = _matmul(h, w_ref[:, cols])
        if scale is None:
            o_ref[:, cols] = p.astype(BF16)
            continue
        if scale not in tables:
            tables[scale] = tuple(t * scale for t in tables[1.0])
        cos, sin_lo, sin_hi = tables[scale]
        for s in range(PROJ_CHUNK // LANES):
            xs = p[:, s * LANES:(s + 1) * LANES]
            rot = (xs * cos
                   + pltpu.roll(xs, LANES - half, axis=1) * sin_lo
                   + pltpu.roll(xs, half, axis=1) * sin_hi)
            lo = c * PROJ_CHUNK + s * LANES
            o_ref[:, lo:lo + LANES] = rot.astype(BF16)


def _attention_maps(q_ref, k_ref, v_ref, bias_ref, o_ref, qt_scr, vt_scr, finalize):
    seq = q_ref.shape[0]
    width = bias_ref.shape[0]
    qt_scr[...] = q_ref[...].astype(F32).T.astype(BF16)
    vt_scr[...] = v_ref[...].astype(F32).T.astype(BF16)
    lo_rows = lax.broadcasted_iota(jnp.int32, (LANES, Q_TILE), 0) < HEAD_DIM
    for t in range(seq // Q_TILE):
        queries = slice(t * Q_TILE, (t + 1) * Q_TILE)
        qt = qt_scr[:, queries]
        outs = []
        for q_half in (jnp.where(lo_rows, qt, 0), jnp.where(lo_rows, 0, qt)):
            m = l = acc = None
            for j in range((t + 1) * Q_TILE // KEY_CHUNK):
                keys = slice(j * KEY_CHUNK, (j + 1) * KEY_CHUNK)
                s = _matmul(k_ref[keys, :], q_half)
                first = width - Q_TILE + j * KEY_CHUNK - t * Q_TILE
                if first >= 0:
                    s = s + bias_ref[first:first + KEY_CHUNK, :]
                m_chunk = jnp.max(s, axis=0, keepdims=True)
                m_new = m_chunk if m is None else jnp.maximum(m, m_chunk)
                p = jnp.exp2(s - m_new)
                p_sum = jnp.sum(p, axis=0, keepdims=True)
                pv = _matmul(vt_scr[:, keys], p.astype(BF16))
                if m is None:
                    l, acc = p_sum, pv
                else:
                    alpha = jnp.exp2(m - m_new)
                    l = alpha * l + p_sum
                    acc = alpha * acc + pv
                m = m_new
            outs.append(acc * (1.0 / l))
        o_ref[queries, :] = finalize(*outs).astype(BF16)


def _diff_attn_kernel(q_ref, k_ref, v_ref, bias_ref, lq1_ref, lk1_ref, lq2_ref, lk2_ref, gain_ref,
                      o_ref, qt_scr, vt_scr, *, lambda_init):
    lam = (jnp.exp(jnp.sum(lq1_ref[...] * lk1_ref[...], axis=-1, keepdims=True))
           - jnp.exp(jnp.sum(lq2_ref[...] * lk2_ref[...], axis=-1, keepdims=True))
           + lambda_init)

    def finalize(lo, hi):
        return _rmsnorm((lo - lam * hi).T, gain_ref[...]) * (1.0 - lambda_init)

    _attention_maps(q_ref, k_ref, v_ref, bias_ref, o_ref, qt_scr, vt_scr, finalize)


def _dil_attn_kernel(q_ref, k_ref, v_ref, bias_ref, o_ref, qt_scr, vt_scr):
    def finalize(lo, hi):
        return jnp.concatenate([lo[0:HEAD_DIM], hi[HEAD_DIM:LANES]], axis=0).T

    _attention_maps(q_ref, k_ref, v_ref, bias_ref, o_ref, qt_scr, vt_scr, finalize)


def _bias_table(width, branches):
    c = np.arange(width)[:, None]
    r = np.arange(Q_TILE)[None, :]
    d = r - c + width - Q_TILE
    count = np.zeros(d.shape, np.float64)
    for span, stride in branches:
        count += (d >= 0) & (d <= span) & (d % stride == 0)
    with np.errstate(divide="ignore"):
        return jnp.asarray(np.log2(count), dtype=F32)


def _row_spec(cols):
    return pl.BlockSpec((ROW_TILE, cols), lambda i: (i, 0))


def _params(*semantics):
    return pltpu.CompilerParams(dimension_semantics=semantics, vmem_limit_bytes=VMEM_LIMIT)


def _ffn1(x, g, wg, wu, wd):
    t, d = x.shape
    f = wg.shape[1]
    return pl.pallas_call(
        _ffn1_kernel,
        grid=(t // ROW_TILE,),
        in_specs=[_row_spec(d), _resident((1, d)), _resident((d, f)), _resident((d, f)),
                  _resident((f, d))],
        out_specs=_row_spec(d),
        out_shape=jax.ShapeDtypeStruct((t, d), F32),
        scratch_shapes=[pltpu.VMEM((ROW_TILE, f), BF16)],
        compiler_params=_params("parallel"),
        name="ffn1",
    )(x, g, wg, wu, wd)


def _in_proj(x, pos, g, w, rope_scale):
    t, d = x.shape
    n = w.shape[1]
    half = ROPE_DIM // 2
    inv = jnp.exp(-math.log(ROPE_THETA) * jnp.arange(half, dtype=F32) * 2.0 / ROPE_DIM)
    pos_tiles = pos.reshape(t // ROW_TILE, ROW_TILE // LANES, LANES)
    return pl.pallas_call(
        functools.partial(_in_proj_kernel, rope_scale=rope_scale),
        grid=(t // ROW_TILE,),
        in_specs=[_row_spec(d),
                  pl.BlockSpec((None, ROW_TILE // LANES, LANES), lambda i: (i, 0, 0)),
                  _resident((1, d)), _resident((d, n)), _resident((half, 1))],
        out_specs=_row_spec(n),
        out_shape=jax.ShapeDtypeStruct((t, n), BF16),
        compiler_params=_params("parallel"),
        name="in_proj",
    )(x, pos_tiles, g, w, inv.reshape(half, 1))


def _attention(body, name, proj, bias, extra, *, q_col, k_col, v_col, groups):
    b, s, _ = proj.shape
    col_spec = lambda first: pl.BlockSpec((None, s, LANES), lambda bi, gi: (bi, 0, first + gi))
    return pl.pallas_call(
        body,
        grid=(b, groups),
        in_specs=[col_spec(q_col), col_spec(k_col), col_spec(v_col), _resident(bias.shape)]
                 + [_resident(e.shape) for e in extra],
        out_specs=pl.BlockSpec((None, s, LANES), lambda bi, gi: (bi, 0, gi)),
        out_shape=jax.ShapeDtypeStruct((b, s, groups * LANES), BF16),
        scratch_shapes=[pltpu.VMEM((LANES, s), BF16)] * 2,
        compiler_params=_params("parallel", "parallel"),
        name=name,
    )(proj, proj, proj, bias, *extra)


def _out_ffn2(x, a, b, wo, g, wg, wu, wd, gf, final_norm):
    t, d = x.shape
    f = wg.shape[1]
    return pl.pallas_call(
        functools.partial(_out_ffn2_kernel, final_norm=final_norm),
        grid=(t // ROW_TILE,),
        in_specs=[_row_spec(d), _row_spec(a.shape[1]), _row_spec(b.shape[1]),
                  _resident(wo.shape), _resident((1, d)), _resident((d, f)), _resident((d, f)),
                  _resident((f, d)), _resident((1, d))],
        out_specs=_row_spec(d),
        out_shape=jax.ShapeDtypeStruct((t, d), F32),
        scratch_shapes=[pltpu.VMEM((ROW_TILE, f), BF16)],
        compiler_params=_params("parallel"),
        name="out_ffn2",
    )(x, a, b, wo, g, wg, wu, wd, gf)


def kernel(x, positions, ffn1_norm, ffn1_gate, ffn1_up, ffn1_down, mix_norm, w_in,
           lambda_q1, lambda_k1, lambda_q2, lambda_k2, subln_gain, w_out,
           ffn2_norm, ffn2_gate, ffn2_up, ffn2_down, final_norm):
    bsz, seq, d = x.shape
    depth = ffn1_norm.shape[0]
    t = bsz * seq
    assert t % ROW_TILE == 0 and seq % Q_TILE == 0 and Q_TILE % KEY_CHUNK == 0
    assert w_in.shape[2] == 3 * (DIFF_WIDTH + DIL_WIDTH) and DIFF_WIDTH % PROJ_CHUNK == 0

    row = lambda v: v.astype(F32).reshape(1, -1)
    q_scale = HEAD_DIM ** -0.5 * math.log2(math.e)
    rope_scale = (q_scale, 1.0, None, q_scale, 1.0, None)
    rope_scale = tuple(s for s in rope_scale for _ in range(DIFF_WIDTH // PROJ_CHUNK))
    groups = DIFF_WIDTH // LANES
    causal_bias = _bias_table(Q_TILE, ((seq, 1),))
    dilated_bias = _bias_table(seq, DIL_CONFIGS)
    pos = positions.reshape(t)

    h = x.reshape(t, d)
    for l in range(depth):
        lambda_init = 0.8 - 0.6 * math.exp(-0.3 * l)
        h = _ffn1(h, row(ffn1_norm[l]), ffn1_gate[l], ffn1_up[l], ffn1_down[l])
        proj = _in_proj(h, pos, row(mix_norm[l]), w_in[l], rope_scale)
        proj = proj.reshape(bsz, seq, -1)
        lams = (row(lambda_q1[l]), row(lambda_k1[l]), row(lambda_q2[l]), row(lambda_k2[l]))
        a = _attention(functools.partial(_diff_attn_kernel, lambda_init=lambda_init), "diff_attn",
                       proj, causal_bias, (*lams, row(subln_gain[l])),
                       q_col=0, k_col=groups, v_col=2 * groups, groups=groups)
        b = _attention(_dil_attn_kernel, "dil_attn", proj, dilated_bias, (),
                       q_col=3 * groups, k_col=4 * groups, v_col=5 * groups, groups=groups)
        h = _out_ffn2(h, a.reshape(t, -1), b.reshape(t, -1), w_out[l], row(ffn2_norm[l]),
                      ffn2_gate[l], ffn2_up[l], ffn2_down[l], row(final_norm),
                      final_norm=(l == depth - 1))
    return h.reshape(bsz, seq, d)
```

```python
import functools
import math

import numpy as np
import jax
import jax.numpy as jnp
from jax import lax
from jax.experimental import pallas as pl
from jax.experimental.pallas import tpu as pltpu

HEAD_DIM = 64
DIFF_HEADS = 4
DIFF_VDIM = 2 * HEAD_DIM
DIL_HEADS = 8
DIL_CONFIGS = ((128, 1), (512, 4), (2048, 16))
DIFF_WIDTH = DIFF_HEADS * DIFF_VDIM
DIL_WIDTH = DIL_HEADS * HEAD_DIM
ROPE_THETA = 500000.0
ROPE_DIM = HEAD_DIM // 4
EPS = 1e-5

LANES = 128
SUBLANES = 8
ROW_TILE = 512
FF_CHUNK = 256
PROJ_CHUNK = 512
Q_TILE = 256
KEY_CHUNK = 128
VMEM_LIMIT = 60 * 1024 * 1024

F32 = jnp.float32
BF16 = jnp.bfloat16
MATMUL_DIMS = (((1,), (0,)), ((), ()))


def _matmul(a, b):
    return lax.dot_general(a, b, MATMUL_DIMS, preferred_element_type=F32)


def _resident(shape):
    return pl.BlockSpec(shape, lambda *_: (0,) * len(shape), pipeline_mode=pl.Buffered(1))


def _rmsnorm(x, g):
    return x * lax.rsqrt(jnp.mean(x * x, axis=-1, keepdims=True) + EPS) * g


def _swiglu_half_step(x, g_ref, wg_ref, wu_ref, wd_ref, act_ref):
    h = _rmsnorm(x, g_ref[...]).astype(BF16)
    d_ff = wg_ref.shape[1]
    for c in range(d_ff // FF_CHUNK):
        cols = slice(c * FF_CHUNK, (c + 1) * FF_CHUNK)
        gate = _matmul(h, wg_ref[:, cols])
        up = _matmul(h, wu_ref[:, cols])
        act_ref[:, cols] = (gate * jax.nn.sigmoid(gate) * up).astype(BF16)
    y = _matmul(act_ref[...], wd_ref[...])
    return x + 0.5 * y


def _ffn1_kernel(x_ref, g_ref, wg_ref, wu_ref, wd_ref, o_ref, act_ref):
    o_ref[...] = _swiglu_half_step(x_ref[...], g_ref, wg_ref, wu_ref, wd_ref, act_ref)


def _out_ffn2_kernel(x_ref, a_ref, b_ref, wo_ref, g_ref, wg_ref, wu_ref, wd_ref, gf_ref,
                     o_ref, act_ref, *, final_norm):
    wa = wo_ref[0:DIFF_WIDTH, :]
    wb = wo_ref[DIFF_WIDTH:DIFF_WIDTH + DIL_WIDTH, :]
    x = x_ref[...] + _matmul(a_ref[...], wa) + _matmul(b_ref[...], wb)
    y = _swiglu_half_step(x, g_ref, wg_ref, wu_ref, wd_ref, act_ref)
    if final_norm:
        y = _rmsnorm(y, gf_ref[...])
    o_ref[...] = y


def _rope_tables(pos_ref, inv_ref):
    half = ROPE_DIM // 2
    pos = pos_ref[...].astype(F32)
    pos_row = jnp.concatenate([pos[a:a + 1, :] for a in range(pos.shape[0])], axis=1)
    ang = inv_ref[...] * pos_row
    cos, sin = jnp.cos(ang), jnp.sin(ang)
    rows = pos_row.shape[1]
    one = jnp.ones((HEAD_DIM - ROPE_DIM, rows), F32)
    zero = jnp.zeros((HEAD_DIM - ROPE_DIM, rows), F32)
    zero_half = jnp.zeros((half, rows), F32)
    heads = LANES // HEAD_DIM
    table = lambda head_rows: jnp.concatenate(head_rows * heads, axis=0).T
    return (table([cos, cos, one]),
            table([-sin, zero_half, zero]),
            table([zero_half, sin, zero]))


def _in_proj_kernel(x_ref, pos_ref, g_ref, w_ref, inv_ref, o_ref, *, rope_scale):
    h = _rmsnorm(x_ref[...], g_ref[...]).astype(BF16)
    tables = {1.0: _rope_tables(pos_ref, inv_ref)}
    half = ROPE_DIM // 2
    for c, scale in enumerate(rope_scale):
        cols = slice(c * PROJ_CHUNK, (c + 1) * PROJ_CHUNK)
        p = _matmul(h, w_ref[:, cols])
        if scale is None:
            o_ref[:, cols] = p.astype(BF16)
            continue
        if scale not in tables:
            tables[scale] = tuple(t * scale for t in tables[1.0])
        cos, sin_lo, sin_hi = tables[scale]
        for s in range(PROJ_CHUNK // LANES):
            xs = p[:, s * LANES:(s + 1) * LANES]
            rot = (xs * cos
                   + pltpu.roll(xs, LANES - half, axis=1) * sin_lo
                   + pltpu.roll(xs, half, axis=1) * sin_hi)
            lo = c * PROJ_CHUNK + s * LANES
            o_ref[:, lo:lo + LANES] = rot.astype(BF16)


def _attention_maps(q_ref, k_ref, v_ref, bias_ref, o_ref, qt_scr, vt_scr, finalize):
    seq = q_ref.shape[0]
    width = bias_ref.shape[0]
    qt_scr[...] = q_ref[...].astype(F32).T.astype(BF16)
    vt_scr[...] = v_ref[...].astype(F32).T.astype(BF16)
    lo_rows = lax.broadcasted_iota(jnp.int32, (LANES, Q_TILE), 0) < HEAD_DIM
    for t in range(seq // Q_TILE):
        queries = slice(t * Q_TILE, (t + 1) * Q_TILE)
        qt = qt_scr[:, queries]
        outs = []
        for q_half in (jnp.where(lo_rows, qt, 0), jnp.where(lo_rows, 0, qt)):
            m = l = acc = None
            for j in range((t + 1) * Q_TILE // KEY_CHUNK):
                keys = slice(j * KEY_CHUNK, (j + 1) * KEY_CHUNK)
                s = _matmul(k_ref[keys, :], q_half)
                first = width - Q_TILE + j * KEY_CHUNK - t * Q_TILE
                if first >= 0:
                    s = s + bias_ref[first:first + KEY_CHUNK, :]
                m_chunk = jnp.max(s, axis=0, keepdims=True)
                m_new = m_chunk if m is None else jnp.maximum(m, m_chunk)
                p = jnp.exp2(s - m_new)
                p_sum = p[0:SUBLANES]
                for r in range(SUBLANES, KEY_CHUNK, SUBLANES):
                    p_sum = p_sum + p[r:r + SUBLANES]
                pv = _matmul(vt_scr[:, keys], p.astype(BF16))
                if m is None:
                    l, acc = p_sum, pv
                else:
                    alpha = jnp.exp2(m - m_new)
                    l = alpha * l + p_sum
                    acc = alpha * acc + pv
                m = m_new
            outs.append(acc * (1.0 / jnp.sum(l, axis=0, keepdims=True)))
        o_ref[queries, :] = finalize(*outs).astype(BF16)


def _diff_attn_kernel(q_ref, k_ref, v_ref, bias_ref, lq1_ref, lk1_ref, lq2_ref, lk2_ref, gain_ref,
                      o_ref, qt_scr, vt_scr, *, lambda_init):
    lam = (jnp.exp(jnp.sum(lq1_ref[...] * lk1_ref[...], axis=-1, keepdims=True))
           - jnp.exp(jnp.sum(lq2_ref[...] * lk2_ref[...], axis=-1, keepdims=True))
           + lambda_init)

    def finalize(lo, hi):
        return _rmsnorm((lo - lam * hi).T, gain_ref[...]) * (1.0 - lambda_init)

    _attention_maps(q_ref, k_ref, v_ref, bias_ref, o_ref, qt_scr, vt_scr, finalize)


def _dil_attn_kernel(q_ref, k_ref, v_ref, bias_ref, o_ref, qt_scr, vt_scr):
    def finalize(lo, hi):
        return jnp.concatenate([lo[0:HEAD_DIM], hi[HEAD_DIM:LANES]], axis=0).T

    _attention_maps(q_ref, k_ref, v_ref, bias_ref, o_ref, qt_scr, vt_scr, finalize)


def _bias_table(width, branches):
    c = np.arange(width)[:, None]
    r = np.arange(Q_TILE)[None, :]
    d = r - c + width - Q_TILE
    count = np.zeros(d.shape, np.float64)
    for span, stride in branches:
        count += (d >= 0) & (d <= span) & (d % stride == 0)
    for first in range(width - Q_TILE, -1, -Q_TILE):
        assert (count[first:first + KEY_CHUNK] > 0).any(axis=0).all()
    with np.errstate(divide="ignore"):
        return jnp.asarray(np.log2(count), dtype=F32)


def _row_spec(cols):
    return pl.BlockSpec((ROW_TILE, cols), lambda i: (i, 0))


def _params(*semantics):
    return pltpu.CompilerParams(dimension_semantics=semantics, vmem_limit_bytes=VMEM_LIMIT)


def _ffn1(x, g, wg, wu, wd):
    t, d = x.shape
    f = wg.shape[1]
    return pl.pallas_call(
        _ffn1_kernel,
        grid=(t // ROW_TILE,),
        in_specs=[_row_spec(d), _resident((1, d)), _resident((d, f)), _resident((d, f)),
                  _resident((f, d))],
        out_specs=_row_spec(d),
        out_shape=jax.ShapeDtypeStruct((t, d), F32),
        scratch_shapes=[pltpu.VMEM((ROW_TILE, f), BF16)],
        compiler_params=_params("parallel"),
        name="ffn1",
    )(x, g, wg, wu, wd)


def _in_proj(x, pos, g, w, rope_scale):
    t, d = x.shape
    n = w.shape[1]
    half = ROPE_DIM // 2
    inv = jnp.exp(-math.log(ROPE_THETA) * jnp.arange(half, dtype=F32) * 2.0 / ROPE_DIM)
    pos_tiles = pos.reshape(t // ROW_TILE, ROW_TILE // LANES, LANES)
    return pl.pallas_call(
        functools.partial(_in_proj_kernel, rope_scale=rope_scale),
        grid=(t // ROW_TILE,),
        in_specs=[_row_spec(d),
                  pl.BlockSpec((None, ROW_TILE // LANES, LANES), lambda i: (i, 0, 0)),
                  _resident((1, d)), _resident((d, n)), _resident((half, 1))],
        out_specs=_row_spec(n),
        out_shape=jax.ShapeDtypeStruct((t, n), BF16),
        compiler_params=_params("parallel"),
        name="in_proj",
    )(x, pos_tiles, g, w, inv.reshape(half, 1))


def _attention(body, name, proj, bias, extra, *, q_col, k_col, v_col, groups):
    b, s, _ = proj.shape
    col_spec = lambda first: pl.BlockSpec((None, s, LANES), lambda bi, gi: (bi, 0, first + gi))
    return pl.pallas_call(
        body,
        grid=(b, groups),
        in_specs=[col_spec(q_col), col_spec(k_col), col_spec(v_col), _resident(bias.shape)]
                 + [_resident(e.shape) for e in extra],
        out_specs=pl.BlockSpec((None, s, LANES), lambda bi, gi: (bi, 0, gi)),
        out_shape=jax.ShapeDtypeStruct((b, s, groups * LANES), BF16),
        scratch_shapes=[pltpu.VMEM((LANES, s), BF16)] * 2,
        compiler_params=_params("parallel", "parallel"),
        name=name,
    )(proj, proj, proj, bias, *extra)


def _out_ffn2(x, a, b, wo, g, wg, wu, wd, gf, final_norm):
    t, d = x.shape
    f = wg.shape[1]
    return pl.pallas_call(
        functools.partial(_out_ffn2_kernel, final_norm=final_norm),
        grid=(t // ROW_TILE,),
        in_specs=[_row_spec(d), _row_spec(a.shape[1]), _row_spec(b.shape[1]),
                  _resident(wo.shape), _resident((1, d)), _resident((d, f)), _resident((d, f)),
                  _resident((f, d)), _resident((1, d))],
        out_specs=_row_spec(d),
        out_shape=jax.ShapeDtypeStruct((t, d), F32),
        scratch_shapes=[pltpu.VMEM((ROW_TILE, f), BF16)],
        compiler_params=_params("parallel"),
        name="out_ffn2",
    )(x, a, b, wo, g, wg, wu, wd, gf)


def kernel(x, positions, ffn1_norm, ffn1_gate, ffn1_up, ffn1_down, mix_norm, w_in,
           lambda_q1, lambda_k1, lambda_q2, lambda_k2, subln_gain, w_out,
           ffn2_norm, ffn2_gate, ffn2_up, ffn2_down, final_norm):
    bsz, seq, d = x.shape
    depth = ffn1_norm.shape[0]
    t = bsz * seq
    assert t % ROW_TILE == 0 and seq % Q_TILE == 0 and Q_TILE % KEY_CHUNK == 0
    assert w_in.shape[2] == 3 * (DIFF_WIDTH + DIL_WIDTH) and DIFF_WIDTH % PROJ_CHUNK == 0

    row = lambda v: v.astype(F32).reshape(1, -1)
    q_scale = HEAD_DIM ** -0.5 * math.log2(math.e)
    rope_scale = (q_scale, 1.0, None, q_scale, 1.0, None)
    rope_scale = tuple(s for s in rope_scale for _ in range(DIFF_WIDTH // PROJ_CHUNK))
    groups = DIFF_WIDTH // LANES
    causal_bias = _bias_table(Q_TILE, ((seq, 1),))
    dilated_bias = _bias_table(seq, DIL_CONFIGS)
    pos = positions.reshape(t)

    h = x.reshape(t, d)
    for l in range(depth):
        lambda_init = 0.8 - 0.6 * math.exp(-0.3 * l)
        h = _ffn1(h, row(ffn1_norm[l]), ffn1_gate[l], ffn1_up[l], ffn1_down[l])
        proj = _in_proj(h, pos, row(mix_norm[l]), w_in[l], rope_scale)
        proj = proj.reshape(bsz, seq, -1)
        lams = (row(lambda_q1[l]), row(lambda_k1[l]), row(lambda_q2[l]), row(lambda_k2[l]))
        a = _attention(functools.partial(_diff_attn_kernel, lambda_init=lambda_init), "diff_attn",
                       proj, causal_bias, (*lams, row(subln_gain[l])),
                       q_col=0, k_col=groups, v_col=2 * groups, groups=groups)
        b = _attention(_dil_attn_kernel, "dil_attn", proj, dilated_bias, (),
                       q_col=3 * groups, k_col=4 * groups, v_col=5 * groups, groups=groups)
        h = _out_ffn2(h, a.reshape(t, -1), b.reshape(t, -1), w_out[l], row(ffn2_norm[l]),
                      ffn2_gate[l], ffn2_up[l], ffn2_down[l], row(final_norm),
                      final_norm=(l == depth - 1))
    return h.reshape(bsz, seq, d)
```

```python
import functools
import math

import numpy as np
import jax
import jax.numpy as jnp
from jax import lax
from jax.experimental import pallas as pl
from jax.experimental.pallas import tpu as pltpu

HEAD_DIM = 64
DIFF_HEADS = 4
DIFF_VDIM = 2 * HEAD_DIM
DIL_HEADS = 8
DIL_CONFIGS = ((128, 1), (512, 4), (2048, 16))
DIFF_WIDTH = DIFF_HEADS * DIFF_VDIM
DIL_WIDTH = DIL_HEADS * HEAD_DIM
ROPE_THETA = 500000.0
ROPE_DIM = HEAD_DIM // 4
EPS = 1e-5

LANES = 128
SUBLANES = 8
ROW_TILE = 512
FF_CHUNK = 256
PROJ_CHUNK = 512
Q_TILE = 256
KEY_CHUNK = 128
ATTN_GROUPS = 2
VMEM_LIMIT = 60 * 1024 * 1024

F32 = jnp.float32
BF16 = jnp.bfloat16
MATMUL_DIMS = (((1,), (0,)), ((), ()))


def _matmul(a, b):
    return lax.dot_general(a, b, MATMUL_DIMS, preferred_element_type=F32)


def _resident(shape):
    return pl.BlockSpec(shape, lambda *_: (0,) * len(shape), pipeline_mode=pl.Buffered(1))


def _rmsnorm(x, g):
    return x * lax.rsqrt(jnp.mean(x * x, axis=-1, keepdims=True) + EPS) * g


def _swiglu_half_step(x, g_ref, wg_ref, wu_ref, wd_ref, act_ref):
    h = _rmsnorm(x, g_ref[...]).astype(BF16)
    d_ff = wg_ref.shape[1]
    for c in range(d_ff // FF_CHUNK):
        cols = slice(c * FF_CHUNK, (c + 1) * FF_CHUNK)
        gate = _matmul(h, wg_ref[:, cols])
        up = _matmul(h, wu_ref[:, cols])
        act_ref[:, cols] = (gate * jax.nn.sigmoid(gate) * up).astype(BF16)
    y = _matmul(act_ref[...], wd_ref[...])
    return x + 0.5 * y


def _ffn1_kernel(x_ref, g_ref, wg_ref, wu_ref, wd_ref, o_ref, act_ref):
    o_ref[...] = _swiglu_half_step(x_ref[...], g_ref, wg_ref, wu_ref, wd_ref, act_ref)


def _out_ffn2_kernel(x_ref, a_ref, b_ref, wo_ref, g_ref, wg_ref, wu_ref, wd_ref, gf_ref,
                     o_ref, act_ref, *, final_norm):
    wa = wo_ref[0:DIFF_WIDTH, :]
    wb = wo_ref[DIFF_WIDTH:DIFF_WIDTH + DIL_WIDTH, :]
    x = x_ref[...] + _matmul(a_ref[...], wa) + _matmul(b_ref[...], wb)
    y = _swiglu_half_step(x, g_ref, wg_ref, wu_ref, wd_ref, act_ref)
    if final_norm:
        y = _rmsnorm(y, gf_ref[...])
    o_ref[...] = y


def _rope_tables(pos_ref, inv_ref):
    half = ROPE_DIM // 2
    pos = pos_ref[...].astype(F32)
    pos_row = jnp.concatenate([pos[a:a + 1, :] for a in range(pos.shape[0])], axis=1)
    ang = inv_ref[...] * pos_row
    cos, sin = jnp.cos(ang), jnp.sin(ang)
    rows = pos_row.shape[1]
    one = jnp.ones((HEAD_DIM - ROPE_DIM, rows), F32)
    zero = jnp.zeros((HEAD_DIM - ROPE_DIM, rows), F32)
    zero_half = jnp.zeros((half, rows), F32)
    heads = LANES // HEAD_DIM
    table = lambda head_rows: jnp.concatenate(head_rows * heads, axis=0).T
    return (table([cos, cos, one]),
            table([-sin, zero_half, zero]),
            table([zero_half, sin, zero]))


def _in_proj_kernel(x_ref, pos_ref, g_ref, w_ref, inv_ref, o_ref, *, rope_scale):
    h = _rmsnorm(x_ref[...], g_ref[...]).astype(BF16)
    tables = {1.0: _rope_tables(pos_ref, inv_ref)}
    half = ROPE_DIM // 2
    for c, scale in enumerate(rope_scale):
        cols = slice(c * PROJ_CHUNK, (c + 1) * PROJ_CHUNK)
        p = _matmul(h, w_ref[:, cols])
        if scale is None:
            o_ref[:, cols] = p.astype(BF16)
            continue
        if scale not in tables:
            tables[scale] = tuple(t * scale for t in tables[1.0])
        cos, sin_lo, sin_hi = tables[scale]
        for s in range(PROJ_CHUNK // LANES):
            xs = p[:, s * LANES:(s + 1) * LANES]
            rot = (xs * cos
                   + pltpu.roll(xs, LANES - half, axis=1) * sin_lo
                   + pltpu.roll(xs, half, axis=1) * sin_hi)
            lo = c * PROJ_CHUNK + s * LANES
            o_ref[:, lo:lo + LANES] = rot.astype(BF16)


def _attention_groups(q_ref, k_ref, v_ref, bias_ref, o_ref, scratch, finalize):
    for g in range(q_ref.shape[1] // LANES):
        lanes = slice(g * LANES, (g + 1) * LANES)
        _attention_maps(q_ref, k_ref, v_ref, bias_ref, o_ref, lanes, scratch[2 * g],
                        scratch[2 * g + 1], finalize)


def _attention_maps(q_ref, k_ref, v_ref, bias_ref, o_ref, lanes, qt_scr, vt_scr, finalize):
    seq = q_ref.shape[0]
    width = bias_ref.shape[0]
    qt_scr[...] = q_ref[:, lanes].astype(F32).T.astype(BF16)
    vt_scr[...] = v_ref[:, lanes].astype(F32).T.astype(BF16)
    lo_rows = lax.broadcasted_iota(jnp.int32, (LANES, Q_TILE), 0) < HEAD_DIM
    for t in range(seq // Q_TILE):
        queries = slice(t * Q_TILE, (t + 1) * Q_TILE)
        qt = qt_scr[:, queries]
        outs = []
        for q_half in (jnp.where(lo_rows, qt, 0), jnp.where(lo_rows, 0, qt)):
            m = l = acc = None
            for j in range((t + 1) * Q_TILE // KEY_CHUNK):
                keys = slice(j * KEY_CHUNK, (j + 1) * KEY_CHUNK)
                s = _matmul(k_ref[keys, lanes], q_half)
                first = width - Q_TILE + j * KEY_CHUNK - t * Q_TILE
                if first >= 0:
                    s = s + bias_ref[first:first + KEY_CHUNK, :]
                m_chunk = jnp.max(s, axis=0, keepdims=True)
                m_new = m_chunk if m is None else jnp.maximum(m, m_chunk)
                p = jnp.exp2(s - m_new)
                p_sum = p[0:SUBLANES]
                for r in range(SUBLANES, KEY_CHUNK, SUBLANES):
                    p_sum = p_sum + p[r:r + SUBLANES]
                pv = _matmul(vt_scr[:, keys], p.astype(BF16))
                if m is None:
                    l, acc = p_sum, pv
                else:
                    alpha = jnp.exp2(m - m_new)
                    l = alpha * l + p_sum
                    acc = alpha * acc + pv
                m = m_new
            outs.append(acc * (1.0 / jnp.sum(l, axis=0, keepdims=True)))
        o_ref[queries, lanes] = finalize(*outs).astype(BF16)


def _diff_attn_kernel(q_ref, k_ref, v_ref, bias_ref, lq1_ref, lk1_ref, lq2_ref, lk2_ref, gain_ref,
                      o_ref, *scratch, lambda_init):
    lam = (jnp.exp(jnp.sum(lq1_ref[...] * lk1_ref[...], axis=-1, keepdims=True))
           - jnp.exp(jnp.sum(lq2_ref[...] * lk2_ref[...], axis=-1, keepdims=True))
           + lambda_init)

    def finalize(lo, hi):
        return _rmsnorm((lo - lam * hi).T, gain_ref[...]) * (1.0 - lambda_init)

    _attention_groups(q_ref, k_ref, v_ref, bias_ref, o_ref, scratch, finalize)


def _dil_attn_kernel(q_ref, k_ref, v_ref, bias_ref, o_ref, *scratch):
    def finalize(lo, hi):
        return jnp.concatenate([lo[0:HEAD_DIM], hi[HEAD_DIM:LANES]], axis=0).T

    _attention_groups(q_ref, k_ref, v_ref, bias_ref, o_ref, scratch, finalize)


def _bias_table(width, branches):
    c = np.arange(width)[:, None]
    r = np.arange(Q_TILE)[None, :]
    d = r - c + width - Q_TILE
    count = np.zeros(d.shape, np.float64)
    for span, stride in branches:
        count += (d >= 0) & (d <= span) & (d % stride == 0)
    for first in range(width - Q_TILE, -1, -Q_TILE):
        assert (count[first:first + KEY_CHUNK] > 0).any(axis=0).all()
    with np.errstate(divide="ignore"):
        return jnp.asarray(np.log2(count), dtype=F32)


def _row_spec(cols):
    return pl.BlockSpec((ROW_TILE, cols), lambda i: (i, 0))


def _params(*semantics):
    return pltpu.CompilerParams(dimension_semantics=semantics, vmem_limit_bytes=VMEM_LIMIT)


def _ffn1(x, g, wg, wu, wd):
    t, d = x.shape
    f = wg.shape[1]
    return pl.pallas_call(
        _ffn1_kernel,
        grid=(t // ROW_TILE,),
        in_specs=[_row_spec(d), _resident((1, d)), _resident((d, f)), _resident((d, f)),
                  _resident((f, d))],
        out_specs=_row_spec(d),
        out_shape=jax.ShapeDtypeStruct((t, d), F32),
        scratch_shapes=[pltpu.VMEM((ROW_TILE, f), BF16)],
        compiler_params=_params("parallel"),
        name="ffn1",
    )(x, g, wg, wu, wd)


def _in_proj(x, pos, g, w, rope_scale):
    t, d = x.shape
    n = w.shape[1]
    half = ROPE_DIM // 2
    inv = jnp.exp(-math.log(ROPE_THETA) * jnp.arange(half, dtype=F32) * 2.0 / ROPE_DIM)
    pos_tiles = pos.reshape(t // ROW_TILE, ROW_TILE // LANES, LANES)
    return pl.pallas_call(
        functools.partial(_in_proj_kernel, rope_scale=rope_scale),
        grid=(t // ROW_TILE,),
        in_specs=[_row_spec(d),
                  pl.BlockSpec((None, ROW_TILE // LANES, LANES), lambda i: (i, 0, 0)),
                  _resident((1, d)), _resident((d, n)), _resident((half, 1))],
        out_specs=_row_spec(n),
        out_shape=jax.ShapeDtypeStruct((t, n), BF16),
        compiler_params=_params("parallel"),
        name="in_proj",
    )(x, pos_tiles, g, w, inv.reshape(half, 1))


def _attention(body, name, proj, bias, extra, *, q_col, k_col, v_col, groups):
    b, s, _ = proj.shape
    assert groups % ATTN_GROUPS == 0 and all(c % ATTN_GROUPS == 0 for c in (q_col, k_col, v_col))
    cols = ATTN_GROUPS * LANES
    col_spec = lambda first: pl.BlockSpec((None, s, cols),
                                          lambda bi, gi: (bi, 0, first // ATTN_GROUPS + gi))
    return pl.pallas_call(
        body,
        grid=(b, groups // ATTN_GROUPS),
        in_specs=[col_spec(q_col), col_spec(k_col), col_spec(v_col), _resident(bias.shape)]
                 + [_resident(e.shape) for e in extra],
        out_specs=pl.BlockSpec((None, s, cols), lambda bi, gi: (bi, 0, gi)),
        out_shape=jax.ShapeDtypeStruct((b, s, groups * LANES), BF16),
        scratch_shapes=[pltpu.VMEM((LANES, s), BF16)] * (2 * ATTN_GROUPS),
        compiler_params=_params("parallel", "parallel"),
        name=name,
    )(proj, proj, proj, bias, *extra)


def _out_ffn2(x, a, b, wo, g, wg, wu, wd, gf, final_norm):
    t, d = x.shape
    f = wg.shape[1]
    return pl.pallas_call(
        functools.partial(_out_ffn2_kernel, final_norm=final_norm),
        grid=(t // ROW_TILE,),
        in_specs=[_row_spec(d), _row_spec(a.shape[1]), _row_spec(b.shape[1]),
                  _resident(wo.shape), _resident((1, d)), _resident((d, f)), _resident((d, f)),
                  _resident((f, d)), _resident((1, d))],
        out_specs=_row_spec(d),
        out_shape=jax.ShapeDtypeStruct((t, d), F32),
        scratch_shapes=[pltpu.VMEM((ROW_TILE, f), BF16)],
        compiler_params=_params("parallel"),
        name="out_ffn2",
    )(x, a, b, wo, g, wg, wu, wd, gf)


def kernel(x, positions, ffn1_norm, ffn1_gate, ffn1_up, ffn1_down, mix_norm, w_in,
           lambda_q1, lambda_k1, lambda_q2, lambda_k2, subln_gain, w_out,
           ffn2_norm, ffn2_gate, ffn2_up, ffn2_down, final_norm):
    bsz, seq, d = x.shape
    depth = ffn1_norm.shape[0]
    t = bsz * seq
    assert t % ROW_TILE == 0 and seq % Q_TILE == 0 and Q_TILE % KEY_CHUNK == 0
    assert w_in.shape[2] == 3 * (DIFF_WIDTH + DIL_WIDTH) and DIFF_WIDTH % PROJ_CHUNK == 0

    row = lambda v: v.astype(F32).reshape(1, -1)
    q_scale = HEAD_DIM ** -0.5 * math.log2(math.e)
    rope_scale = (q_scale, 1.0, None, q_scale, 1.0, None)
    rope_scale = tuple(s for s in rope_scale for _ in range(DIFF_WIDTH // PROJ_CHUNK))
    groups = DIFF_WIDTH // LANES
    causal_bias = _bias_table(Q_TILE, ((seq, 1),))
    dilated_bias = _bias_table(seq, DIL_CONFIGS)
    pos = positions.reshape(t)

    h = x.reshape(t, d)
    for l in range(depth):
        lambda_init = 0.8 - 0.6 * math.exp(-0.3 * l)
        h = _ffn1(h, row(ffn1_norm[l]), ffn1_gate[l], ffn1_up[l], ffn1_down[l])
        proj = _in_proj(h, pos, row(mix_norm[l]), w_in[l], rope_scale)
        proj = proj.reshape(bsz, seq, -1)
        lams = (row(lambda_q1[l]), row(lambda_k1[l]), row(lambda_q2[l]), row(lambda_k2[l]))
        a = _attention(functools.partial(_diff_attn_kernel, lambda_init=lambda_init), "diff_attn",
                       proj, causal_bias, (*lams, row(subln_gain[l])),
                       q_col=0, k_col=groups, v_col=2 * groups, groups=groups)
        b = _attention(_dil_attn_kernel, "dil_attn", proj, dilated_bias, (),
                       q_col=3 * groups, k_col=4 * groups, v_col=5 * groups, groups=groups)
        h = _out_ffn2(h, a.reshape(t, -1), b.reshape(t, -1), w_out[l], row(ffn2_norm[l]),
                      ffn2_gate[l], ffn2_up[l], ffn2_down[l], row(final_norm),
                      final_norm=(l == depth - 1))
    return h.reshape(bsz, seq, d)
```

```python
import functools
import math

import numpy as np
import jax
import jax.numpy as jnp
from jax import lax
from jax.experimental import pallas as pl
from jax.experimental.pallas import tpu as pltpu

HEAD_DIM = 64
DIFF_HEADS = 4
DIFF_VDIM = 2 * HEAD_DIM
DIL_HEADS = 8
DIL_CONFIGS = ((128, 1), (512, 4), (2048, 16))
DIFF_WIDTH = DIFF_HEADS * DIFF_VDIM
DIL_WIDTH = DIL_HEADS * HEAD_DIM
ROPE_THETA = 500000.0
ROPE_DIM = HEAD_DIM // 4
EPS = 1e-5

LANES = 128
SUBLANES = 8
ROW_TILE = 512
WIDE_ROW_TILE = 1024
FF_CHUNK = 256
PROJ_CHUNK = 512
Q_TILE = 256
KEY_CHUNK = 128
ATTN_GROUPS = 2
VMEM_LIMIT = 60 * 1024 * 1024

F32 = jnp.float32
BF16 = jnp.bfloat16
MATMUL_DIMS = (((1,), (0,)), ((), ()))


def _matmul(a, b):
    return lax.dot_general(a, b, MATMUL_DIMS, preferred_element_type=F32)


def _resident(shape):
    return pl.BlockSpec(shape, lambda *_: (0,) * len(shape), pipeline_mode=pl.Buffered(1))


def _rmsnorm(x, g):
    return x * lax.rsqrt(jnp.mean(x * x, axis=-1, keepdims=True) + EPS) * g


def _swiglu_half_step(x, g_ref, wg_ref, wu_ref, wd_ref, act_ref):
    h = _rmsnorm(x, g_ref[...]).astype(BF16)
    d_ff = wg_ref.shape[1]
    for c in range(d_ff // FF_CHUNK):
        cols = slice(c * FF_CHUNK, (c + 1) * FF_CHUNK)
        gate = _matmul(h, wg_ref[:, cols])
        up = _matmul(h, wu_ref[:, cols])
        act_ref[:, cols] = (gate * jax.nn.sigmoid(gate) * up).astype(BF16)
    y = _matmul(act_ref[...], wd_ref[...])
    return x + 0.5 * y


def _ffn1_kernel(x_ref, g_ref, wg_ref, wu_ref, wd_ref, o_ref, act_ref):
    o_ref[...] = _swiglu_half_step(x_ref[...], g_ref, wg_ref, wu_ref, wd_ref, act_ref)


def _out_ffn2_kernel(x_ref, a_ref, b_ref, wo_ref, g_ref, wg_ref, wu_ref, wd_ref, gf_ref,
                     o_ref, act_ref, *, final_norm):
    wa = wo_ref[0:DIFF_WIDTH, :]
    wb = wo_ref[DIFF_WIDTH:DIFF_WIDTH + DIL_WIDTH, :]
    x = x_ref[...] + _matmul(a_ref[...], wa) + _matmul(b_ref[...], wb)
    y = _swiglu_half_step(x, g_ref, wg_ref, wu_ref, wd_ref, act_ref)
    if final_norm:
        y = _rmsnorm(y, gf_ref[...])
    o_ref[...] = y


def _rope_tables(pos_ref, inv_ref):
    half = ROPE_DIM // 2
    pos = pos_ref[...].astype(F32)
    pos_row = jnp.concatenate([pos[a:a + 1, :] for a in range(pos.shape[0])], axis=1)
    ang = inv_ref[...] * pos_row
    cos, sin = jnp.cos(ang), jnp.sin(ang)
    rows = pos_row.shape[1]
    one = jnp.ones((HEAD_DIM - ROPE_DIM, rows), F32)
    zero = jnp.zeros((HEAD_DIM - ROPE_DIM, rows), F32)
    zero_half = jnp.zeros((half, rows), F32)
    heads = LANES // HEAD_DIM
    table = lambda head_rows: jnp.concatenate(head_rows * heads, axis=0).T
    return (table([cos, cos, one]),
            table([-sin, zero_half, zero]),
            table([zero_half, sin, zero]))


def _in_proj_kernel(x_ref, pos_ref, g_ref, w_ref, inv_ref, o_ref, *, rope_scale):
    h = _rmsnorm(x_ref[...], g_ref[...]).astype(BF16)
    tables = {1.0: _rope_tables(pos_ref, inv_ref)}
    half = ROPE_DIM // 2
    for c, scale in enumerate(rope_scale):
        cols = slice(c * PROJ_CHUNK, (c + 1) * PROJ_CHUNK)
        p = _matmul(h, w_ref[:, cols])
        if scale is None:
            o_ref[:, cols] = p.astype(BF16)
            continue
        if scale not in tables:
            tables[scale] = tuple(t * scale for t in tables[1.0])
        cos, sin_lo, sin_hi = tables[scale]
        for s in range(PROJ_CHUNK // LANES):
            xs = p[:, s * LANES:(s + 1) * LANES]
            rot = (xs * cos
                   + pltpu.roll(xs, LANES - half, axis=1) * sin_lo
                   + pltpu.roll(xs, half, axis=1) * sin_hi)
            lo = c * PROJ_CHUNK + s * LANES
            o_ref[:, lo:lo + LANES] = rot.astype(BF16)


def _attention_groups(q_ref, k_ref, v_ref, bias_ref, o_ref, scratch, finalize):
    for g in range(q_ref.shape[1] // LANES):
        lanes = slice(g * LANES, (g + 1) * LANES)
        _attention_maps(q_ref, k_ref, v_ref, bias_ref, o_ref, lanes, scratch[2 * g],
                        scratch[2 * g + 1], finalize)


def _attention_maps(q_ref, k_ref, v_ref, bias_ref, o_ref, lanes, qt_scr, vt_scr, finalize):
    seq = q_ref.shape[0]
    width = bias_ref.shape[0]
    qt_scr[...] = q_ref[:, lanes].astype(F32).T.astype(BF16)
    vt_scr[...] = v_ref[:, lanes].astype(F32).T.astype(BF16)
    lo_rows = lax.broadcasted_iota(jnp.int32, (LANES, Q_TILE), 0) < HEAD_DIM
    for t in range(seq // Q_TILE):
        queries = slice(t * Q_TILE, (t + 1) * Q_TILE)
        qt = qt_scr[:, queries]
        outs = []
        for q_half in (jnp.where(lo_rows, qt, 0), jnp.where(lo_rows, 0, qt)):
            m = l = acc = None
            for j in range((t + 1) * Q_TILE // KEY_CHUNK):
                keys = slice(j * KEY_CHUNK, (j + 1) * KEY_CHUNK)
                s = _matmul(k_ref[keys, lanes], q_half)
                first = width - Q_TILE + j * KEY_CHUNK - t * Q_TILE
                if first >= 0:
                    s = s + bias_ref[first:first + KEY_CHUNK, :]
                m_chunk = jnp.max(s, axis=0, keepdims=True)
                m_new = m_chunk if m is None else jnp.maximum(m, m_chunk)
                p = jnp.exp2(s - m_new)
                p_sum = p[0:SUBLANES]
                for r in range(SUBLANES, KEY_CHUNK, SUBLANES):
                    p_sum = p_sum + p[r:r + SUBLANES]
                pv = _matmul(vt_scr[:, keys], p.astype(BF16))
                if m is None:
                    l, acc = p_sum, pv
                else:
                    alpha = jnp.exp2(m - m_new)
                    l = alpha * l + p_sum
                    acc = alpha * acc + pv
                m = m_new
            outs.append(acc * (1.0 / jnp.sum(l, axis=0, keepdims=True)))
        o_ref[queries, lanes] = finalize(*outs).astype(BF16)


def _diff_attn_kernel(q_ref, k_ref, v_ref, bias_ref, lq1_ref, lk1_ref, lq2_ref, lk2_ref, gain_ref,
                      o_ref, *scratch, lambda_init):
    lam = (jnp.exp(jnp.sum(lq1_ref[...] * lk1_ref[...], axis=-1, keepdims=True))
           - jnp.exp(jnp.sum(lq2_ref[...] * lk2_ref[...], axis=-1, keepdims=True))
           + lambda_init)

    def finalize(lo, hi):
        return _rmsnorm((lo - lam * hi).T, gain_ref[...]) * (1.0 - lambda_init)

    _attention_groups(q_ref, k_ref, v_ref, bias_ref, o_ref, scratch, finalize)


def _dil_attn_kernel(q_ref, k_ref, v_ref, bias_ref, o_ref, *scratch):
    def finalize(lo, hi):
        return jnp.concatenate([lo[0:HEAD_DIM], hi[HEAD_DIM:LANES]], axis=0).T

    _attention_groups(q_ref, k_ref, v_ref, bias_ref, o_ref, scratch, finalize)


def _bias_table(width, branches):
    c = np.arange(width)[:, None]
    r = np.arange(Q_TILE)[None, :]
    d = r - c + width - Q_TILE
    count = np.zeros(d.shape, np.float64)
    for span, stride in branches:
        count += (d >= 0) & (d <= span) & (d % stride == 0)
    for first in range(width - Q_TILE, -1, -Q_TILE):
        assert (count[first:first + KEY_CHUNK] > 0).any(axis=0).all()
    with np.errstate(divide="ignore"):
        return jnp.asarray(np.log2(count), dtype=F32)


def _row_spec(cols, rows=ROW_TILE):
    return pl.BlockSpec((rows, cols), lambda i: (i, 0))


def _params(*semantics):
    return pltpu.CompilerParams(dimension_semantics=semantics, vmem_limit_bytes=VMEM_LIMIT)


def _ffn1(x, g, wg, wu, wd):
    t, d = x.shape
    f = wg.shape[1]
    return pl.pallas_call(
        _ffn1_kernel,
        grid=(t // WIDE_ROW_TILE,),
        in_specs=[_row_spec(d, WIDE_ROW_TILE), _resident((1, d)), _resident((d, f)),
                  _resident((d, f)), _resident((f, d))],
        out_specs=_row_spec(d, WIDE_ROW_TILE),
        out_shape=jax.ShapeDtypeStruct((t, d), F32),
        scratch_shapes=[pltpu.VMEM((WIDE_ROW_TILE, f), BF16)],
        compiler_params=_params("parallel"),
        name="ffn1",
    )(x, g, wg, wu, wd)


def _in_proj(x, pos, g, w, rope_scale):
    t, d = x.shape
    n = w.shape[1]
    half = ROPE_DIM // 2
    inv = jnp.exp(-math.log(ROPE_THETA) * jnp.arange(half, dtype=F32) * 2.0 / ROPE_DIM)
    rows = WIDE_ROW_TILE
    pos_tiles = pos.reshape(t // rows, rows // LANES, LANES)
    return pl.pallas_call(
        functools.partial(_in_proj_kernel, rope_scale=rope_scale),
        grid=(t // rows,),
        in_specs=[_row_spec(d, rows),
                  pl.BlockSpec((None, rows // LANES, LANES), lambda i: (i, 0, 0)),
                  _resident((1, d)), _resident((d, n)), _resident((half, 1))],
        out_specs=_row_spec(n, rows),
        out_shape=jax.ShapeDtypeStruct((t, n), BF16),
        compiler_params=_params("parallel"),
        name="in_proj",
    )(x, pos_tiles, g, w, inv.reshape(half, 1))


def _attention(body, name, proj, bias, extra, *, q_col, k_col, v_col, groups):
    b, s, _ = proj.shape
    assert groups % ATTN_GROUPS == 0 and all(c % ATTN_GROUPS == 0 for c in (q_col, k_col, v_col))
    cols = ATTN_GROUPS * LANES
    col_spec = lambda first: pl.BlockSpec((None, s, cols),
                                          lambda bi, gi: (bi, 0, first // ATTN_GROUPS + gi))
    return pl.pallas_call(
        body,
        grid=(b, groups // ATTN_GROUPS),
        in_specs=[col_spec(q_col), col_spec(k_col), col_spec(v_col), _resident(bias.shape)]
                 + [_resident(e.shape) for e in extra],
        out_specs=pl.BlockSpec((None, s, cols), lambda bi, gi: (bi, 0, gi)),
        out_shape=jax.ShapeDtypeStruct((b, s, groups * LANES), BF16),
        scratch_shapes=[pltpu.VMEM((LANES, s), BF16)] * (2 * ATTN_GROUPS),
        compiler_params=_params("parallel", "parallel"),
        name=name,
    )(proj, proj, proj, bias, *extra)


def _out_ffn2(x, a, b, wo, g, wg, wu, wd, gf, final_norm):
    t, d = x.shape
    f = wg.shape[1]
    return pl.pallas_call(
        functools.partial(_out_ffn2_kernel, final_norm=final_norm),
        grid=(t // ROW_TILE,),
        in_specs=[_row_spec(d), _row_spec(a.shape[1]), _row_spec(b.shape[1]),
                  _resident(wo.shape), _resident((1, d)), _resident((d, f)), _resident((d, f)),
                  _resident((f, d)), _resident((1, d))],
        out_specs=_row_spec(d),
        out_shape=jax.ShapeDtypeStruct((t, d), F32),
        scratch_shapes=[pltpu.VMEM((ROW_TILE, f), BF16)],
        compiler_params=_params("parallel"),
        name="out_ffn2",
    )(x, a, b, wo, g, wg, wu, wd, gf)


def kernel(x, positions, ffn1_norm, ffn1_gate, ffn1_up, ffn1_down, mix_norm, w_in,
           lambda_q1, lambda_k1, lambda_q2, lambda_k2, subln_gain, w_out,
           ffn2_norm, ffn2_gate, ffn2_up, ffn2_down, final_norm):
    bsz, seq, d = x.shape
    depth = ffn1_norm.shape[0]
    t = bsz * seq
    assert t % WIDE_ROW_TILE == 0 and WIDE_ROW_TILE % ROW_TILE == 0
    assert seq % Q_TILE == 0 and Q_TILE % KEY_CHUNK == 0
    assert w_in.shape[2] == 3 * (DIFF_WIDTH + DIL_WIDTH) and DIFF_WIDTH % PROJ_CHUNK == 0

    row = lambda v: v.astype(F32).reshape(1, -1)
    q_scale = HEAD_DIM ** -0.5 * math.log2(math.e)
    rope_scale = (q_scale, 1.0, None, q_scale, 1.0, None)
    rope_scale = tuple(s for s in rope_scale for _ in range(DIFF_WIDTH // PROJ_CHUNK))
    groups = DIFF_WIDTH // LANES
    causal_bias = _bias_table(Q_TILE, ((seq, 1),))
    dilated_bias = _bias_table(seq, DIL_CONFIGS)
    pos = positions.reshape(t)

    h = x.reshape(t, d)
    for l in range(depth):
        lambda_init = 0.8 - 0.6 * math.exp(-0.3 * l)
        h = _ffn1(h, row(ffn1_norm[l]), ffn1_gate[l], ffn1_up[l], ffn1_down[l])
        proj = _in_proj(h, pos, row(mix_norm[l]), w_in[l], rope_scale)
        proj = proj.reshape(bsz, seq, -1)
        lams = (row(lambda_q1[l]), row(lambda_k1[l]), row(lambda_q2[l]), row(lambda_k2[l]))
        a = _attention(functools.partial(_diff_attn_kernel, lambda_init=lambda_init), "diff_attn",
                       proj, causal_bias, (*lams, row(subln_gain[l])),
                       q_col=0, k_col=groups, v_col=2 * groups, groups=groups)
        b = _attention(_dil_attn_kernel, "dil_attn", proj, dilated_bias, (),
                       q_col=3 * groups, k_col=4 * groups, v_col=5 * groups, groups=groups)
        h = _out_ffn2(h, a.reshape(t, -1), b.reshape(t, -1), w_out[l], row(ffn2_norm[l]),
                      ffn2_gate[l], ffn2_up[l], ffn2_down[l], row(final_norm),
                      final_norm=(l == depth - 1))
    return h.reshape(bsz, seq, d)
```

```python
import functools
import math

import numpy as np
import jax
import jax.numpy as jnp
from jax import lax
from jax.experimental import pallas as pl
from jax.experimental.pallas import tpu as pltpu

HEAD_DIM = 64
DIFF_HEADS = 4
DIFF_VDIM = 2 * HEAD_DIM
DIL_HEADS = 8
DIL_CONFIGS = ((128, 1), (512, 4), (2048, 16))
DIFF_WIDTH = DIFF_HEADS * DIFF_VDIM
DIL_WIDTH = DIL_HEADS * HEAD_DIM
ROPE_THETA = 500000.0
ROPE_DIM = HEAD_DIM // 4
EPS = 1e-5

LANES = 128
SUBLANES = 8
ROW_TILE = 512
WIDE_ROW_TILE = 1024
FF_CHUNK = 256
PROJ_CHUNK = 512
Q_TILE = 256
KEY_CHUNK = 128
ATTN_GROUPS = 2
VMEM_LIMIT = 60 * 1024 * 1024

F32 = jnp.float32
BF16 = jnp.bfloat16
MATMUL_DIMS = (((1,), (0,)), ((), ()))


def _matmul(a, b):
    return lax.dot_general(a, b, MATMUL_DIMS, preferred_element_type=F32)


def _resident(shape):
    return pl.BlockSpec(shape, lambda *_: (0,) * len(shape), pipeline_mode=pl.Buffered(1))


def _rmsnorm(x, g):
    return x * lax.rsqrt(jnp.mean(x * x, axis=-1, keepdims=True) + EPS) * g


def _swiglu_half_step(x, g_ref, wg_ref, wu_ref, wd_ref, act_ref):
    h = _rmsnorm(x, g_ref[...]).astype(BF16)
    d_ff = wg_ref.shape[1]
    for c in range(d_ff // FF_CHUNK):
        cols = slice(c * FF_CHUNK, (c + 1) * FF_CHUNK)
        gate = _matmul(h, wg_ref[:, cols])
        up = _matmul(h, wu_ref[:, cols])
        act_ref[:, cols] = (gate * jax.nn.sigmoid(gate) * up).astype(BF16)
    y = _matmul(act_ref[...], wd_ref[...])
    return x + 0.5 * y


def _ffn1_kernel(x_ref, g_ref, wg_ref, wu_ref, wd_ref, o_ref, act_ref):
    o_ref[...] = _swiglu_half_step(x_ref[...], g_ref, wg_ref, wu_ref, wd_ref, act_ref)


def _out_ffn2_kernel(x_ref, a_ref, b_ref, wo_ref, g_ref, wg_ref, wu_ref, wd_ref, gf_ref,
                     o_ref, act_ref, *, final_norm):
    wa = wo_ref[0:DIFF_WIDTH, :]
    wb = wo_ref[DIFF_WIDTH:DIFF_WIDTH + DIL_WIDTH, :]
    x = x_ref[...] + _matmul(a_ref[...], wa) + _matmul(b_ref[...], wb)
    y = _swiglu_half_step(x, g_ref, wg_ref, wu_ref, wd_ref, act_ref)
    if final_norm:
        y = _rmsnorm(y, gf_ref[...])
    o_ref[...] = y


def _rope_tables(pos_ref, inv_ref):
    half = ROPE_DIM // 2
    pos = pos_ref[...].astype(F32)
    pos_row = jnp.concatenate([pos[a:a + 1, :] for a in range(pos.shape[0])], axis=1)
    ang = inv_ref[...] * pos_row
    cos, sin = jnp.cos(ang), jnp.sin(ang)
    rows = pos_row.shape[1]
    one = jnp.ones((HEAD_DIM - ROPE_DIM, rows), F32)
    zero = jnp.zeros((HEAD_DIM - ROPE_DIM, rows), F32)
    zero_half = jnp.zeros((half, rows), F32)
    heads = LANES // HEAD_DIM
    table = lambda head_rows: jnp.concatenate(head_rows * heads, axis=0).T
    return (table([cos, cos, one]),
            table([-sin, zero_half, zero]),
            table([zero_half, sin, zero]))


def _in_proj_kernel(x_ref, pos_ref, g_ref, w_ref, inv_ref, o_ref, *, rope_scale):
    h = _rmsnorm(x_ref[...], g_ref[...]).astype(BF16)
    tables = {1.0: _rope_tables(pos_ref, inv_ref)}
    half = ROPE_DIM // 2
    for c, scale in enumerate(rope_scale):
        cols = slice(c * PROJ_CHUNK, (c + 1) * PROJ_CHUNK)
        p = _matmul(h, w_ref[:, cols])
        if scale is None:
            o_ref[:, cols] = p.astype(BF16)
            continue
        if scale not in tables:
            tables[scale] = tuple(t * scale for t in tables[1.0])
        cos, sin_lo, sin_hi = tables[scale]
        for s in range(PROJ_CHUNK // LANES):
            xs = p[:, s * LANES:(s + 1) * LANES]
            rot = (xs * cos
                   + pltpu.roll(xs, LANES - half, axis=1) * sin_lo
                   + pltpu.roll(xs, half, axis=1) * sin_hi)
            lo = c * PROJ_CHUNK + s * LANES
            o_ref[:, lo:lo + LANES] = rot.astype(BF16)


def _attention_groups(q_ref, k_ref, v_ref, bias_ref, o_ref, scratch, finalize):
    for g in range(q_ref.shape[1] // LANES):
        lanes = slice(g * LANES, (g + 1) * LANES)
        _attention_maps(q_ref, k_ref, v_ref, bias_ref, o_ref, lanes, scratch[2 * g],
                        scratch[2 * g + 1], finalize)


def _attention_maps(q_ref, k_ref, v_ref, bias_ref, o_ref, lanes, qt_scr, vt_scr, finalize):
    seq = q_ref.shape[0]
    width = bias_ref.shape[0]
    qt_scr[...] = q_ref[:, lanes].astype(F32).T.astype(BF16)
    vt_scr[...] = v_ref[:, lanes].astype(F32).T.astype(BF16)
    lo_rows = lax.broadcasted_iota(jnp.int32, (LANES, Q_TILE), 0) < HEAD_DIM
    for t in range(seq // Q_TILE):
        queries = slice(t * Q_TILE, (t + 1) * Q_TILE)
        qt = qt_scr[:, queries]
        outs = []
        for q_half in (jnp.where(lo_rows, qt, 0), jnp.where(lo_rows, 0, qt)):
            m = l = acc = None
            for j in range((t + 1) * Q_TILE // KEY_CHUNK):
                keys = slice(j * KEY_CHUNK, (j + 1) * KEY_CHUNK)
                skip = max(0, j * KEY_CHUNK - t * Q_TILE)
                live = slice(skip, Q_TILE)
                s = _matmul(k_ref[keys, lanes], q_half[:, live])
                first = width - Q_TILE + j * KEY_CHUNK - t * Q_TILE
                if first >= 0:
                    s = s + bias_ref[first:first + KEY_CHUNK, live]
                m_chunk = jnp.max(s, axis=0, keepdims=True)
                m_new = m_chunk if m is None else jnp.maximum(m[:, live], m_chunk)
                p = jnp.exp2(s - m_new)
                p_sum = p[0:SUBLANES]
                for r in range(SUBLANES, KEY_CHUNK, SUBLANES):
                    p_sum = p_sum + p[r:r + SUBLANES]
                pv = _matmul(vt_scr[:, keys], p.astype(BF16))
                if m is None:
                    m, l, acc = m_new, p_sum, pv
                else:
                    alpha = jnp.exp2(m[:, live] - m_new)
                    new = (m_new, alpha * l[:, live] + p_sum, alpha * acc[:, live] + pv)
                    if skip:
                        new = [jnp.concatenate([old[:, 0:skip], upd], axis=1)
                               for old, upd in zip((m, l, acc), new)]
                    m, l, acc = new
            outs.append(acc * (1.0 / jnp.sum(l, axis=0, keepdims=True)))
        o_ref[queries, lanes] = finalize(*outs).astype(BF16)


def _diff_attn_kernel(q_ref, k_ref, v_ref, bias_ref, lq1_ref, lk1_ref, lq2_ref, lk2_ref, gain_ref,
                      o_ref, *scratch, lambda_init):
    lam = (jnp.exp(jnp.sum(lq1_ref[...] * lk1_ref[...], axis=-1, keepdims=True))
           - jnp.exp(jnp.sum(lq2_ref[...] * lk2_ref[...], axis=-1, keepdims=True))
           + lambda_init)

    def finalize(lo, hi):
        return _rmsnorm((lo - lam * hi).T, gain_ref[...]) * (1.0 - lambda_init)

    _attention_groups(q_ref, k_ref, v_ref, bias_ref, o_ref, scratch, finalize)


def _dil_attn_kernel(q_ref, k_ref, v_ref, bias_ref, o_ref, *scratch):
    def finalize(lo, hi):
        return jnp.concatenate([lo[0:HEAD_DIM], hi[HEAD_DIM:LANES]], axis=0).T

    _attention_groups(q_ref, k_ref, v_ref, bias_ref, o_ref, scratch, finalize)


def _bias_table(width, branches):
    c = np.arange(width)[:, None]
    r = np.arange(Q_TILE)[None, :]
    d = r - c + width - Q_TILE
    count = np.zeros(d.shape, np.float64)
    for span, stride in branches:
        count += (d >= 0) & (d <= span) & (d % stride == 0)
    for first in range(width - Q_TILE, -1, -Q_TILE):
        assert (count[first:first + KEY_CHUNK] > 0).any(axis=0).all()
    with np.errstate(divide="ignore"):
        return jnp.asarray(np.log2(count), dtype=F32)


def _row_spec(cols, rows=ROW_TILE):
    return pl.BlockSpec((rows, cols), lambda i: (i, 0))


def _params(*semantics):
    return pltpu.CompilerParams(dimension_semantics=semantics, vmem_limit_bytes=VMEM_LIMIT)


def _ffn1(x, g, wg, wu, wd):
    t, d = x.shape
    f = wg.shape[1]
    return pl.pallas_call(
        _ffn1_kernel,
        grid=(t // WIDE_ROW_TILE,),
        in_specs=[_row_spec(d, WIDE_ROW_TILE), _resident((1, d)), _resident((d, f)),
                  _resident((d, f)), _resident((f, d))],
        out_specs=_row_spec(d, WIDE_ROW_TILE),
        out_shape=jax.ShapeDtypeStruct((t, d), F32),
        scratch_shapes=[pltpu.VMEM((WIDE_ROW_TILE, f), BF16)],
        compiler_params=_params("parallel"),
        name="ffn1",
    )(x, g, wg, wu, wd)


def _in_proj(x, pos, g, w, rope_scale):
    t, d = x.shape
    n = w.shape[1]
    half = ROPE_DIM // 2
    inv = jnp.exp(-math.log(ROPE_THETA) * jnp.arange(half, dtype=F32) * 2.0 / ROPE_DIM)
    rows = WIDE_ROW_TILE
    pos_tiles = pos.reshape(t // rows, rows // LANES, LANES)
    return pl.pallas_call(
        functools.partial(_in_proj_kernel, rope_scale=rope_scale),
        grid=(t // rows,),
        in_specs=[_row_spec(d, rows),
                  pl.BlockSpec((None, rows // LANES, LANES), lambda i: (i, 0, 0)),
                  _resident((1, d)), _resident((d, n)), _resident((half, 1))],
        out_specs=_row_spec(n, rows),
        out_shape=jax.ShapeDtypeStruct((t, n), BF16),
        compiler_params=_params("parallel"),
        name="in_proj",
    )(x, pos_tiles, g, w, inv.reshape(half, 1))


def _attention(body, name, proj, bias, extra, *, q_col, k_col, v_col, groups):
    b, s, _ = proj.shape
    assert groups % ATTN_GROUPS == 0 and all(c % ATTN_GROUPS == 0 for c in (q_col, k_col, v_col))
    cols = ATTN_GROUPS * LANES
    col_spec = lambda first: pl.BlockSpec((None, s, cols),
                                          lambda bi, gi: (bi, 0, first // ATTN_GROUPS + gi))
    return pl.pallas_call(
        body,
        grid=(b, groups // ATTN_GROUPS),
        in_specs=[col_spec(q_col), col_spec(k_col), col_spec(v_col), _resident(bias.shape)]
                 + [_resident(e.shape) for e in extra],
        out_specs=pl.BlockSpec((None, s, cols), lambda bi, gi: (bi, 0, gi)),
        out_shape=jax.ShapeDtypeStruct((b, s, groups * LANES), BF16),
        scratch_shapes=[pltpu.VMEM((LANES, s), BF16)] * (2 * ATTN_GROUPS),
        compiler_params=_params("parallel", "parallel"),
        name=name,
    )(proj, proj, proj, bias, *extra)


def _out_ffn2(x, a, b, wo, g, wg, wu, wd, gf, final_norm):
    t, d = x.shape
    f = wg.shape[1]
    return pl.pallas_call(
        functools.partial(_out_ffn2_kernel, final_norm=final_norm),
        grid=(t // ROW_TILE,),
        in_specs=[_row_spec(d), _row_spec(a.shape[1]), _row_spec(b.shape[1]),
                  _resident(wo.shape), _resident((1, d)), _resident((d, f)), _resident((d, f)),
                  _resident((f, d)), _resident((1, d))],
        out_specs=_row_spec(d),
        out_shape=jax.ShapeDtypeStruct((t, d), F32),
        scratch_shapes=[pltpu.VMEM((ROW_TILE, f), BF16)],
        compiler_params=_params("parallel"),
        name="out_ffn2",
    )(x, a, b, wo, g, wg, wu, wd, gf)


def kernel(x, positions, ffn1_norm, ffn1_gate, ffn1_up, ffn1_down, mix_norm, w_in,
           lambda_q1, lambda_k1, lambda_q2, lambda_k2, subln_gain, w_out,
           ffn2_norm, ffn2_gate, ffn2_up, ffn2_down, final_norm):
    bsz, seq, d = x.shape
    depth = ffn1_norm.shape[0]
    t = bsz * seq
    assert t % WIDE_ROW_TILE == 0 and WIDE_ROW_TILE % ROW_TILE == 0
    assert seq % Q_TILE == 0 and Q_TILE % KEY_CHUNK == 0
    assert w_in.shape[2] == 3 * (DIFF_WIDTH + DIL_WIDTH) and DIFF_WIDTH % PROJ_CHUNK == 0

    row = lambda v: v.astype(F32).reshape(1, -1)
    q_scale = HEAD_DIM ** -0.5 * math.log2(math.e)
    rope_scale = (q_scale, 1.0, None, q_scale, 1.0, None)
    rope_scale = tuple(s for s in rope_scale for _ in range(DIFF_WIDTH // PROJ_CHUNK))
    groups = DIFF_WIDTH // LANES
    causal_bias = _bias_table(Q_TILE, ((seq, 1),))
    dilated_bias = _bias_table(seq, DIL_CONFIGS)
    pos = positions.reshape(t)

    h = x.reshape(t, d)
    for l in range(depth):
        lambda_init = 0.8 - 0.6 * math.exp(-0.3 * l)
        h = _ffn1(h, row(ffn1_norm[l]), ffn1_gate[l], ffn1_up[l], ffn1_down[l])
        proj = _in_proj(h, pos, row(mix_norm[l]), w_in[l], rope_scale)
        proj = proj.reshape(bsz, seq, -1)
        lams = (row(lambda_q1[l]), row(lambda_k1[l]), row(lambda_q2[l]), row(lambda_k2[l]))
        a = _attention(functools.partial(_diff_attn_kernel, lambda_init=lambda_init), "diff_attn",
                       proj, causal_bias, (*lams, row(subln_gain[l])),
                       q_col=0, k_col=groups, v_col=2 * groups, groups=groups)
        b = _attention(_dil_attn_kernel, "dil_attn", proj, dilated_bias, (),
                       q_col=3 * groups, k_col=4 * groups, v_col=5 * groups, groups=groups)
        h = _out_ffn2(h, a.reshape(t, -1), b.reshape(t, -1), w_out[l], row(ffn2_norm[l]),
                      ffn2_gate[l], ffn2_up[l], ffn2_down[l], row(final_norm),
                      final_norm=(l == depth - 1))
    return h.reshape(bsz, seq, d)
```

```python
import functools
import math

import numpy as np
import jax
import jax.numpy as jnp
from jax import lax
from jax.experimental import pallas as pl
from jax.experimental.pallas import tpu as pltpu

HEAD_DIM = 64
DIFF_HEADS = 4
DIFF_VDIM = 2 * HEAD_DIM
DIL_HEADS = 8
DIL_CONFIGS = ((128, 1), (512, 4), (2048, 16))
DIFF_WIDTH = DIFF_HEADS * DIFF_VDIM
DIL_WIDTH = DIL_HEADS * HEAD_DIM
ROPE_THETA = 500000.0
ROPE_DIM = HEAD_DIM // 4
EPS = 1e-5

LANES = 128
SUBLANES = 8
ROW_TILE = 1024
FF_CHUNK = 256
PROJ_CHUNK = 512
Q_TILE = 256
KEY_CHUNK = 128
ATTN_GROUPS = 2
VMEM_LIMIT = 60 * 1024 * 1024

F32 = jnp.float32
BF16 = jnp.bfloat16
MATMUL_DIMS = (((1,), (0,)), ((), ()))


def _matmul(a, b):
    return lax.dot_general(a, b, MATMUL_DIMS, preferred_element_type=F32)


def _resident(shape):
    return pl.BlockSpec(shape, lambda *_: (0,) * len(shape), pipeline_mode=pl.Buffered(1))


def _rmsnorm(x, g):
    return x * lax.rsqrt(jnp.mean(x * x, axis=-1, keepdims=True) + EPS) * g


def _swiglu_half_step(x, g_ref, wg_ref, wu_ref, wd_ref, act_ref):
    h = _rmsnorm(x, g_ref[...]).astype(BF16)
    d_ff = wg_ref.shape[1]
    for c in range(d_ff // FF_CHUNK):
        cols = slice(c * FF_CHUNK, (c + 1) * FF_CHUNK)
        gate = _matmul(h, wg_ref[:, cols])
        up = _matmul(h, wu_ref[:, cols])
        act_ref[:, cols] = (gate * jax.nn.sigmoid(gate) * up).astype(BF16)
    y = _matmul(act_ref[...], wd_ref[...])
    return x + 0.5 * y


def _ffn_kernel(x_ref, g_ref, wg_ref, wu_ref, wd_ref, gf_ref, o_ref, act_ref, *, final_norm):
    y = _swiglu_half_step(x_ref[...], g_ref, wg_ref, wu_ref, wd_ref, act_ref)
    if final_norm:
        y = _rmsnorm(y, gf_ref[...])
    o_ref[...] = y


def _out_proj_kernel(x_ref, a_ref, b_ref, wo_ref, o_ref):
    wa = wo_ref[0:DIFF_WIDTH, :]
    wb = wo_ref[DIFF_WIDTH:DIFF_WIDTH + DIL_WIDTH, :]
    o_ref[...] = x_ref[...] + _matmul(a_ref[...], wa) + _matmul(b_ref[...], wb)


def _rope_tables(pos_ref, inv_ref):
    half = ROPE_DIM // 2
    pos = pos_ref[...].astype(F32)
    pos_row = jnp.concatenate([pos[a:a + 1, :] for a in range(pos.shape[0])], axis=1)
    ang = inv_ref[...] * pos_row
    cos, sin = jnp.cos(ang), jnp.sin(ang)
    rows = pos_row.shape[1]
    one = jnp.ones((HEAD_DIM - ROPE_DIM, rows), F32)
    zero = jnp.zeros((HEAD_DIM - ROPE_DIM, rows), F32)
    zero_half = jnp.zeros((half, rows), F32)
    heads = LANES // HEAD_DIM
    table = lambda head_rows: jnp.concatenate(head_rows * heads, axis=0).T
    return (table([cos, cos, one]),
            table([-sin, zero_half, zero]),
            table([zero_half, sin, zero]))


def _in_proj_kernel(x_ref, pos_ref, g_ref, w_ref, inv_ref, o_ref, *, rope_scale):
    h = _rmsnorm(x_ref[...], g_ref[...]).astype(BF16)
    tables = {1.0: _rope_tables(pos_ref, inv_ref)}
    half = ROPE_DIM // 2
    for c, scale in enumerate(rope_scale):
        cols = slice(c * PROJ_CHUNK, (c + 1) * PROJ_CHUNK)
        p = _matmul(h, w_ref[:, cols])
        if scale is None:
            o_ref[:, cols] = p.astype(BF16)
            continue
        if scale not in tables:
            tables[scale] = tuple(t * scale for t in tables[1.0])
        cos, sin_lo, sin_hi = tables[scale]
        for s in range(PROJ_CHUNK // LANES):
            xs = p[:, s * LANES:(s + 1) * LANES]
            rot = (xs * cos
                   + pltpu.roll(xs, LANES - half, axis=1) * sin_lo
                   + pltpu.roll(xs, half, axis=1) * sin_hi)
            lo = c * PROJ_CHUNK + s * LANES
            o_ref[:, lo:lo + LANES] = rot.astype(BF16)


def _attention_groups(q_ref, k_ref, v_ref, bias_ref, o_ref, scratch, finalize):
    for g in range(q_ref.shape[1] // LANES):
        lanes = slice(g * LANES, (g + 1) * LANES)
        _attention_maps(q_ref, k_ref, v_ref, bias_ref, o_ref, lanes, scratch[2 * g],
                        scratch[2 * g + 1], finalize)


def _attention_maps(q_ref, k_ref, v_ref, bias_ref, o_ref, lanes, qt_scr, vt_scr, finalize):
    seq = q_ref.shape[0]
    width = bias_ref.shape[0]
    qt_scr[...] = q_ref[:, lanes].astype(F32).T.astype(BF16)
    vt_scr[...] = v_ref[:, lanes].astype(F32).T.astype(BF16)
    lo_rows = lax.broadcasted_iota(jnp.int32, (LANES, Q_TILE), 0) < HEAD_DIM
    for t in range(seq // Q_TILE):
        queries = slice(t * Q_TILE, (t + 1) * Q_TILE)
        qt = qt_scr[:, queries]
        outs = []
        for q_half in (jnp.where(lo_rows, qt, 0), jnp.where(lo_rows, 0, qt)):
            m = l = acc = None
            for j in range((t + 1) * Q_TILE // KEY_CHUNK):
                keys = slice(j * KEY_CHUNK, (j + 1) * KEY_CHUNK)
                skip = max(0, j * KEY_CHUNK - t * Q_TILE)
                live = slice(skip, Q_TILE)
                s = _matmul(k_ref[keys, lanes], q_half[:, live])
                first = width - Q_TILE + j * KEY_CHUNK - t * Q_TILE
                if first >= 0:
                    s = s + bias_ref[first:first + KEY_CHUNK, live]
                m_chunk = jnp.max(s, axis=0, keepdims=True)
                m_new = m_chunk if m is None else jnp.maximum(m[:, live], m_chunk)
                p = jnp.exp2(s - m_new)
                p_sum = p[0:SUBLANES]
                for r in range(SUBLANES, KEY_CHUNK, SUBLANES):
                    p_sum = p_sum + p[r:r + SUBLANES]
                pv = _matmul(vt_scr[:, keys], p.astype(BF16))
                if m is None:
                    m, l, acc = m_new, p_sum, pv
                else:
                    alpha = jnp.exp2(m[:, live] - m_new)
                    new = (m_new, alpha * l[:, live] + p_sum, alpha * acc[:, live] + pv)
                    if skip:
                        new = [jnp.concatenate([old[:, 0:skip], upd], axis=1)
                               for old, upd in zip((m, l, acc), new)]
                    m, l, acc = new
            outs.append(acc * (1.0 / jnp.sum(l, axis=0, keepdims=True)))
        o_ref[queries, lanes] = finalize(*outs).astype(BF16)


def _diff_attn_kernel(q_ref, k_ref, v_ref, bias_ref, lq1_ref, lk1_ref, lq2_ref, lk2_ref, gain_ref,
                      o_ref, *scratch, lambda_init):
    lam = (jnp.exp(jnp.sum(lq1_ref[...] * lk1_ref[...], axis=-1, keepdims=True))
           - jnp.exp(jnp.sum(lq2_ref[...] * lk2_ref[...], axis=-1, keepdims=True))
           + lambda_init)

    def finalize(lo, hi):
        return _rmsnorm((lo - lam * hi).T, gain_ref[...]) * (1.0 - lambda_init)

    _attention_groups(q_ref, k_ref, v_ref, bias_ref, o_ref, scratch, finalize)


def _dil_attn_kernel(q_ref, k_ref, v_ref, bias_ref, o_ref, *scratch):
    def finalize(lo, hi):
        return jnp.concatenate([lo[0:HEAD_DIM], hi[HEAD_DIM:LANES]], axis=0).T

    _attention_groups(q_ref, k_ref, v_ref, bias_ref, o_ref, scratch, finalize)


def _bias_table(width, branches):
    c = np.arange(width)[:, None]
    r = np.arange(Q_TILE)[None, :]
    d = r - c + width - Q_TILE
    count = np.zeros(d.shape, np.float64)
    for span, stride in branches:
        count += (d >= 0) & (d <= span) & (d % stride == 0)
    for first in range(width - Q_TILE, -1, -Q_TILE):
        assert (count[first:first + KEY_CHUNK] > 0).any(axis=0).all()
    with np.errstate(divide="ignore"):
        return jnp.asarray(np.log2(count), dtype=F32)


def _row_spec(cols):
    return pl.BlockSpec((ROW_TILE, cols), lambda i: (i, 0))


def _params(*semantics):
    return pltpu.CompilerParams(dimension_semantics=semantics, vmem_limit_bytes=VMEM_LIMIT)


def _ffn(name, x, g, wg, wu, wd, gf, final_norm):
    t, d = x.shape
    f = wg.shape[1]
    return pl.pallas_call(
        functools.partial(_ffn_kernel, final_norm=final_norm),
        grid=(t // ROW_TILE,),
        in_specs=[_row_spec(d), _resident((1, d)), _resident((d, f)), _resident((d, f)),
                  _resident((f, d)), _resident((1, d))],
        out_specs=_row_spec(d),
        out_shape=jax.ShapeDtypeStruct((t, d), F32),
        scratch_shapes=[pltpu.VMEM((ROW_TILE, f), BF16)],
        compiler_params=_params("parallel"),
        name=name,
    )(x, g, wg, wu, wd, gf)


def _in_proj(x, pos, g, w, rope_scale):
    t, d = x.shape
    n = w.shape[1]
    half = ROPE_DIM // 2
    inv = jnp.exp(-math.log(ROPE_THETA) * jnp.arange(half, dtype=F32) * 2.0 / ROPE_DIM)
    pos_tiles = pos.reshape(t // ROW_TILE, ROW_TILE // LANES, LANES)
    return pl.pallas_call(
        functools.partial(_in_proj_kernel, rope_scale=rope_scale),
        grid=(t // ROW_TILE,),
        in_specs=[_row_spec(d),
                  pl.BlockSpec((None, ROW_TILE // LANES, LANES), lambda i: (i, 0, 0)),
                  _resident((1, d)), _resident((d, n)), _resident((half, 1))],
        out_specs=_row_spec(n),
        out_shape=jax.ShapeDtypeStruct((t, n), BF16),
        compiler_params=_params("parallel"),
        name="in_proj",
    )(x, pos_tiles, g, w, inv.reshape(half, 1))


def _attention(body, name, proj, bias, extra, *, q_col, k_col, v_col, groups):
    b, s, _ = proj.shape
    assert groups % ATTN_GROUPS == 0 and all(c % ATTN_GROUPS == 0 for c in (q_col, k_col, v_col))
    cols = ATTN_GROUPS * LANES
    col_spec = lambda first: pl.BlockSpec((None, s, cols),
                                          lambda bi, gi: (bi, 0, first // ATTN_GROUPS + gi))
    return pl.pallas_call(
        body,
        grid=(b, groups // ATTN_GROUPS),
        in_specs=[col_spec(q_col), col_spec(k_col), col_spec(v_col), _resident(bias.shape)]
                 + [_resident(e.shape) for e in extra],
        out_specs=pl.BlockSpec((None, s, cols), lambda bi, gi: (bi, 0, gi)),
        out_shape=jax.ShapeDtypeStruct((b, s, groups * LANES), BF16),
        scratch_shapes=[pltpu.VMEM((LANES, s), BF16)] * (2 * ATTN_GROUPS),
        compiler_params=_params("parallel", "parallel"),
        name=name,
    )(proj, proj, proj, bias, *extra)


def _out_proj(x, a, b, wo):
    t, d = x.shape
    return pl.pallas_call(
        _out_proj_kernel,
        grid=(t // ROW_TILE,),
        in_specs=[_row_spec(d), _row_spec(a.shape[1]), _row_spec(b.shape[1]),
                  _resident(wo.shape)],
        out_specs=_row_spec(d),
        out_shape=jax.ShapeDtypeStruct((t, d), F32),
        compiler_params=_params("parallel"),
        name="out_proj",
    )(x, a, b, wo)


def kernel(x, positions, ffn1_norm, ffn1_gate, ffn1_up, ffn1_down, mix_norm, w_in,
           lambda_q1, lambda_k1, lambda_q2, lambda_k2, subln_gain, w_out,
           ffn2_norm, ffn2_gate, ffn2_up, ffn2_down, final_norm):
    bsz, seq, d = x.shape
    depth = ffn1_norm.shape[0]
    t = bsz * seq
    assert t % ROW_TILE == 0 and seq % Q_TILE == 0 and Q_TILE % KEY_CHUNK == 0
    assert w_in.shape[2] == 3 * (DIFF_WIDTH + DIL_WIDTH) and DIFF_WIDTH % PROJ_CHUNK == 0

    row = lambda v: v.astype(F32).reshape(1, -1)
    q_scale = HEAD_DIM ** -0.5 * math.log2(math.e)
    rope_scale = (q_scale, 1.0, None, q_scale, 1.0, None)
    rope_scale = tuple(s for s in rope_scale for _ in range(DIFF_WIDTH // PROJ_CHUNK))
    groups = DIFF_WIDTH // LANES
    causal_bias = _bias_table(Q_TILE, ((seq, 1),))
    dilated_bias = _bias_table(seq, DIL_CONFIGS)
    pos = positions.reshape(t)

    h = x.reshape(t, d)
    for l in range(depth):
        lambda_init = 0.8 - 0.6 * math.exp(-0.3 * l)
        h = _ffn("ffn1", h, row(ffn1_norm[l]), ffn1_gate[l], ffn1_up[l], ffn1_down[l],
                 row(final_norm), final_norm=False)
        proj = _in_proj(h, pos, row(mix_norm[l]), w_in[l], rope_scale)
        proj = proj.reshape(bsz, seq, -1)
        lams = (row(lambda_q1[l]), row(lambda_k1[l]), row(lambda_q2[l]), row(lambda_k2[l]))
        a = _attention(functools.partial(_diff_attn_kernel, lambda_init=lambda_init), "diff_attn",
                       proj, causal_bias, (*lams, row(subln_gain[l])),
                       q_col=0, k_col=groups, v_col=2 * groups, groups=groups)
        b = _attention(_dil_attn_kernel, "dil_attn", proj, dilated_bias, (),
                       q_col=3 * groups, k_col=4 * groups, v_col=5 * groups, groups=groups)
        h = _out_proj(h, a.reshape(t, -1), b.reshape(t, -1), w_out[l])
        h = _ffn("ffn2", h, row(ffn2_norm[l]), ffn2_gate[l], ffn2_up[l], ffn2_down[l],
                 row(final_norm), final_norm=(l == depth - 1))
    return h.reshape(bsz, seq, d)
```

```python
import functools
import math

import numpy as np
import jax
import jax.numpy as jnp
from jax import lax
from jax.experimental import pallas as pl
from jax.experimental.pallas import tpu as pltpu

HEAD_DIM = 64
DIFF_HEADS = 4
DIFF_VDIM = 2 * HEAD_DIM
DIL_HEADS = 8
DIL_CONFIGS = ((128, 1), (512, 4), (2048, 16))
DIFF_WIDTH = DIFF_HEADS * DIFF_VDIM
DIL_WIDTH = DIL_HEADS * HEAD_DIM
ROPE_THETA = 500000.0
ROPE_DIM = HEAD_DIM // 4
EPS = 1e-5

LANES = 128
SUBLANES = 8
ROW_TILE = 512
WIDE_ROW_TILE = 1024
FF_CHUNK = 256
PROJ_CHUNK = 512
Q_TILE = 256
KEY_CHUNK = 128
ATTN_GROUPS = 2
VMEM_LIMIT = 60 * 1024 * 1024

F32 = jnp.float32
BF16 = jnp.bfloat16
MATMUL_DIMS = (((1,), (0,)), ((), ()))


def _matmul(a, b):
    return lax.dot_general(a, b, MATMUL_DIMS, preferred_element_type=F32)


def _resident(shape):
    return pl.BlockSpec(shape, lambda *_: (0,) * len(shape), pipeline_mode=pl.Buffered(1))


def _rmsnorm(x, g):
    return x * lax.rsqrt(jnp.mean(x * x, axis=-1, keepdims=True) + EPS) * g


def _swiglu_half_step(x, g_ref, wg_ref, wu_ref, wd_ref, act_ref):
    h = _rmsnorm(x, g_ref[...]).astype(BF16)
    d_ff = wg_ref.shape[1]
    for c in range(d_ff // FF_CHUNK):
        cols = slice(c * FF_CHUNK, (c + 1) * FF_CHUNK)
        gate = _matmul(h, wg_ref[:, cols])
        up = _matmul(h, wu_ref[:, cols])
        act_ref[:, cols] = (gate * jax.nn.sigmoid(gate) * up).astype(BF16)
    y = _matmul(act_ref[...], wd_ref[...])
    return x + 0.5 * y


def _ffn1_kernel(x_ref, g_ref, wg_ref, wu_ref, wd_ref, o_ref, act_ref):
    o_ref[...] = _swiglu_half_step(x_ref[...], g_ref, wg_ref, wu_ref, wd_ref, act_ref)


def _out_ffn2_kernel(x_ref, a_ref, b_ref, wo_ref, g_ref, wg_ref, wu_ref, wd_ref, gf_ref,
                     o_ref, act_ref, *, final_norm):
    wa = wo_ref[0:DIFF_WIDTH, :]
    wb = wo_ref[DIFF_WIDTH:DIFF_WIDTH + DIL_WIDTH, :]
    x = x_ref[...] + _matmul(a_ref[...], wa) + _matmul(b_ref[...], wb)
    y = _swiglu_half_step(x, g_ref, wg_ref, wu_ref, wd_ref, act_ref)
    if final_norm:
        y = _rmsnorm(y, gf_ref[...])
    o_ref[...] = y


def _rope_tables(pos_ref, inv_ref):
    half = ROPE_DIM // 2
    pos = pos_ref[...].astype(F32)
    pos_row = jnp.concatenate([pos[a:a + 1, :] for a in range(pos.shape[0])], axis=1)
    ang = inv_ref[...] * pos_row
    cos, sin = jnp.cos(ang), jnp.sin(ang)
    rows = pos_row.shape[1]
    one = jnp.ones((HEAD_DIM - ROPE_DIM, rows), F32)
    zero = jnp.zeros((HEAD_DIM - ROPE_DIM, rows), F32)
    zero_half = jnp.zeros((half, rows), F32)
    heads = LANES // HEAD_DIM
    table = lambda head_rows: jnp.concatenate(head_rows * heads, axis=0).T
    return (table([cos, cos, one]),
            table([-sin, zero_half, zero]),
            table([zero_half, sin, zero]))


def _in_proj_kernel(x_ref, pos_ref, g_ref, w_ref, inv_ref, o_ref, *, rope_scale):
    h = _rmsnorm(x_ref[...], g_ref[...]).astype(BF16)
    tables = {1.0: _rope_tables(pos_ref, inv_ref)}
    half = ROPE_DIM // 2
    for c, scale in enumerate(rope_scale):
        cols = slice(c * PROJ_CHUNK, (c + 1) * PROJ_CHUNK)
        p = _matmul(h, w_ref[:, cols])
        if scale is None:
            o_ref[:, cols] = p.astype(BF16)
            continue
        if scale not in tables:
            tables[scale] = tuple(t * scale for t in tables[1.0])
        cos, sin_lo, sin_hi = tables[scale]
        for s in range(PROJ_CHUNK // LANES):
            xs = p[:, s * LANES:(s + 1) * LANES]
            rot = (xs * cos
                   + pltpu.roll(xs, LANES - half, axis=1) * sin_lo
                   + pltpu.roll(xs, half, axis=1) * sin_hi)
            lo = c * PROJ_CHUNK + s * LANES
            o_ref[:, lo:lo + LANES] = rot.astype(BF16)


def _attention_groups(q_ref, k_ref, v_ref, bias_ref, o_ref, scratch, finalize):
    for g in range(q_ref.shape[1] // LANES):
        lanes = slice(g * LANES, (g + 1) * LANES)
        _attention_maps(q_ref, k_ref, v_ref, bias_ref, o_ref, lanes, scratch[2 * g],
                        scratch[2 * g + 1], finalize)


def _attention_maps(q_ref, k_ref, v_ref, bias_ref, o_ref, lanes, qt_scr, vt_scr, finalize):
    seq = q_ref.shape[0]
    width = bias_ref.shape[0]
    qt_scr[...] = q_ref[:, lanes].astype(F32).T.astype(BF16)
    vt_scr[...] = v_ref[:, lanes].astype(F32).T.astype(BF16)
    lo_rows = lax.broadcasted_iota(jnp.int32, (LANES, Q_TILE), 0) < HEAD_DIM
    for t in range(seq // Q_TILE):
        queries = slice(t * Q_TILE, (t + 1) * Q_TILE)
        qt = qt_scr[:, queries]
        outs = []
        for q_half in (jnp.where(lo_rows, qt, 0), jnp.where(lo_rows, 0, qt)):
            m = l = acc = None
            for j in range((t + 1) * Q_TILE // KEY_CHUNK):
                keys = slice(j * KEY_CHUNK, (j + 1) * KEY_CHUNK)
                skip = max(0, j * KEY_CHUNK - t * Q_TILE)
                live = slice(skip, Q_TILE)
                s = _matmul(k_ref[keys, lanes], q_half[:, live])
                first = width - Q_TILE + j * KEY_CHUNK - t * Q_TILE
                if first >= 0:
                    s = s + bias_ref[first:first + KEY_CHUNK, live]
                m_chunk = jnp.max(s, axis=0, keepdims=True)
                m_new = m_chunk if m is None else jnp.maximum(m[:, live], m_chunk)
                p = jnp.exp2(s - m_new)
                p_sum = p[0:SUBLANES]
                for r in range(SUBLANES, KEY_CHUNK, SUBLANES):
                    p_sum = p_sum + p[r:r + SUBLANES]
                pv = _matmul(vt_scr[:, keys], p.astype(BF16))
                if m is None:
                    m, l, acc = m_new, p_sum, pv
                else:
                    alpha = jnp.exp2(m[:, live] - m_new)
                    new = (m_new, alpha * l[:, live] + p_sum, alpha * acc[:, live] + pv)
                    if skip:
                        new = [jnp.concatenate([old[:, 0:skip], upd], axis=1)
                               for old, upd in zip((m, l, acc), new)]
                    m, l, acc = new
            outs.append(acc * (1.0 / jnp.sum(l, axis=0, keepdims=True)))
        o_ref[queries, lanes] = finalize(*outs).astype(BF16)


def _mixers_kernel(qa_ref, ka_ref, va_ref, qb_ref, kb_ref, vb_ref, causal_ref, dilated_ref,
                   lq1_ref, lk1_ref, lq2_ref, lk2_ref, gain_ref, a_ref, b_ref, *scratch,
                   lambda_init):
    lam = (jnp.exp(jnp.sum(lq1_ref[...] * lk1_ref[...], axis=-1, keepdims=True))
           - jnp.exp(jnp.sum(lq2_ref[...] * lk2_ref[...], axis=-1, keepdims=True))
           + lambda_init)

    def differential(lo, hi):
        return _rmsnorm((lo - lam * hi).T, gain_ref[...]) * (1.0 - lambda_init)

    def two_heads(lo, hi):
        return jnp.concatenate([lo[0:HEAD_DIM], hi[HEAD_DIM:LANES]], axis=0).T

    half = len(scratch) // 2
    _attention_groups(qa_ref, ka_ref, va_ref, causal_ref, a_ref, scratch[:half], differential)
    _attention_groups(qb_ref, kb_ref, vb_ref, dilated_ref, b_ref, scratch[half:], two_heads)


def _bias_table(width, branches):
    c = np.arange(width)[:, None]
    r = np.arange(Q_TILE)[None, :]
    d = r - c + width - Q_TILE
    count = np.zeros(d.shape, np.float64)
    for span, stride in branches:
        count += (d >= 0) & (d <= span) & (d % stride == 0)
    for first in range(width - Q_TILE, -1, -Q_TILE):
        assert (count[first:first + KEY_CHUNK] > 0).any(axis=0).all()
    with np.errstate(divide="ignore"):
        return jnp.asarray(np.log2(count), dtype=F32)


def _row_spec(cols, rows=ROW_TILE):
    return pl.BlockSpec((rows, cols), lambda i: (i, 0))


def _params(*semantics):
    return pltpu.CompilerParams(dimension_semantics=semantics, vmem_limit_bytes=VMEM_LIMIT)


def _ffn1(x, g, wg, wu, wd):
    t, d = x.shape
    f = wg.shape[1]
    return pl.pallas_call(
        _ffn1_kernel,
        grid=(t // WIDE_ROW_TILE,),
        in_specs=[_row_spec(d, WIDE_ROW_TILE), _resident((1, d)), _resident((d, f)),
                  _resident((d, f)), _resident((f, d))],
        out_specs=_row_spec(d, WIDE_ROW_TILE),
        out_shape=jax.ShapeDtypeStruct((t, d), F32),
        scratch_shapes=[pltpu.VMEM((WIDE_ROW_TILE, f), BF16)],
        compiler_params=_params("parallel"),
        name="ffn1",
    )(x, g, wg, wu, wd)


def _in_proj(x, pos, g, w, rope_scale):
    t, d = x.shape
    n = w.shape[1]
    half = ROPE_DIM // 2
    inv = jnp.exp(-math.log(ROPE_THETA) * jnp.arange(half, dtype=F32) * 2.0 / ROPE_DIM)
    rows = WIDE_ROW_TILE
    pos_tiles = pos.reshape(t // rows, rows // LANES, LANES)
    return pl.pallas_call(
        functools.partial(_in_proj_kernel, rope_scale=rope_scale),
        grid=(t // rows,),
        in_specs=[_row_spec(d, rows),
                  pl.BlockSpec((None, rows // LANES, LANES), lambda i: (i, 0, 0)),
                  _resident((1, d)), _resident((d, n)), _resident((half, 1))],
        out_specs=_row_spec(n, rows),
        out_shape=jax.ShapeDtypeStruct((t, n), BF16),
        compiler_params=_params("parallel"),
        name="in_proj",
    )(x, pos_tiles, g, w, inv.reshape(half, 1))


def _mixers(proj, causal_bias, dilated_bias, lams, gain, groups, lambda_init):
    b, s, _ = proj.shape
    assert groups % ATTN_GROUPS == 0
    cols = ATTN_GROUPS * LANES
    steps = groups // ATTN_GROUPS
    col_spec = lambda part: pl.BlockSpec((None, s, cols), lambda bi, gi: (bi, 0, part * steps + gi))
    out_spec = pl.BlockSpec((None, s, cols), lambda bi, gi: (bi, 0, gi))
    out_shape = jax.ShapeDtypeStruct((b, s, groups * LANES), BF16)
    extra = (causal_bias, dilated_bias, *lams, gain)
    return pl.pallas_call(
        functools.partial(_mixers_kernel, lambda_init=lambda_init),
        grid=(b, steps),
        in_specs=[col_spec(part) for part in range(6)] + [_resident(e.shape) for e in extra],
        out_specs=[out_spec, out_spec],
        out_shape=[out_shape, out_shape],
        scratch_shapes=[pltpu.VMEM((LANES, s), BF16)] * (4 * ATTN_GROUPS),
        compiler_params=_params("parallel", "parallel"),
        name="mixers",
    )(*([proj] * 6), *extra)


def _out_ffn2(x, a, b, wo, g, wg, wu, wd, gf, final_norm):
    t, d = x.shape
    f = wg.shape[1]
    return pl.pallas_call(
        functools.partial(_out_ffn2_kernel, final_norm=final_norm),
        grid=(t // ROW_TILE,),
        in_specs=[_row_spec(d), _row_spec(a.shape[1]), _row_spec(b.shape[1]),
                  _resident(wo.shape), _resident((1, d)), _resident((d, f)), _resident((d, f)),
                  _resident((f, d)), _resident((1, d))],
        out_specs=_row_spec(d),
        out_shape=jax.ShapeDtypeStruct((t, d), F32),
        scratch_shapes=[pltpu.VMEM((ROW_TILE, f), BF16)],
        compiler_params=_params("parallel"),
        name="out_ffn2",
    )(x, a, b, wo, g, wg, wu, wd, gf)


def kernel(x, positions, ffn1_norm, ffn1_gate, ffn1_up, ffn1_down, mix_norm, w_in,
           lambda_q1, lambda_k1, lambda_q2, lambda_k2, subln_gain, w_out,
           ffn2_norm, ffn2_gate, ffn2_up, ffn2_down, final_norm):
    bsz, seq, d = x.shape
    depth = ffn1_norm.shape[0]
    t = bsz * seq
    assert t % WIDE_ROW_TILE == 0 and WIDE_ROW_TILE % ROW_TILE == 0
    assert seq % Q_TILE == 0 and Q_TILE % KEY_CHUNK == 0
    assert w_in.shape[2] == 3 * (DIFF_WIDTH + DIL_WIDTH) and DIFF_WIDTH == DIL_WIDTH
    assert DIFF_WIDTH % PROJ_CHUNK == 0

    row = lambda v: v.astype(F32).reshape(1, -1)
    q_scale = HEAD_DIM ** -0.5 * math.log2(math.e)
    rope_scale = (q_scale, 1.0, None, q_scale, 1.0, None)
    rope_scale = tuple(s for s in rope_scale for _ in range(DIFF_WIDTH // PROJ_CHUNK))
    groups = DIFF_WIDTH // LANES
    causal_bias = _bias_table(Q_TILE, ((seq, 1),))
    dilated_bias = _bias_table(seq, DIL_CONFIGS)
    pos = positions.reshape(t)

    h = x.reshape(t, d)
    for l in range(depth):
        lambda_init = 0.8 - 0.6 * math.exp(-0.3 * l)
        h = _ffn1(h, row(ffn1_norm[l]), ffn1_gate[l], ffn1_up[l], ffn1_down[l])
        proj = _in_proj(h, pos, row(mix_norm[l]), w_in[l], rope_scale)
        proj = proj.reshape(bsz, seq, -1)
        lams = (row(lambda_q1[l]), row(lambda_k1[l]), row(lambda_q2[l]), row(lambda_k2[l]))
        a, b = _mixers(proj, causal_bias, dilated_bias, lams, row(subln_gain[l]), groups,
                       lambda_init)
        h = _out_ffn2(h, a.reshape(t, -1), b.reshape(t, -1), w_out[l], row(ffn2_norm[l]),
                      ffn2_gate[l], ffn2_up[l], ffn2_down[l], row(final_norm),
                      final_norm=(l == depth - 1))
    return h.reshape(bsz, seq, d)
```

```python
import functools
import math

import numpy as np
import jax
import jax.numpy as jnp
from jax import lax
from jax.experimental import pallas as pl
from jax.experimental.pallas import tpu as pltpu

HEAD_DIM = 64
DIFF_HEADS = 4
DIFF_VDIM = 2 * HEAD_DIM
DIL_HEADS = 8
DIL_CONFIGS = ((128, 1), (512, 4), (2048, 16))
DIFF_WIDTH = DIFF_HEADS * DIFF_VDIM
DIL_WIDTH = DIL_HEADS * HEAD_DIM
ROPE_THETA = 500000.0
ROPE_DIM = HEAD_DIM // 4
EPS = 1e-5

LANES = 128
SUBLANES = 8
ROW_TILE = 512
WIDE_ROW_TILE = 1024
FF_CHUNK = 256
PROJ_CHUNK = 512
Q_TILE = 256
KEY_CHUNK = 128
ATTN_GROUPS = 2
VMEM_LIMIT = 60 * 1024 * 1024

F32 = jnp.float32
BF16 = jnp.bfloat16
MATMUL_DIMS = (((1,), (0,)), ((), ()))


def _matmul(a, b):
    return lax.dot_general(a, b, MATMUL_DIMS, preferred_element_type=F32)


def _resident(shape):
    return pl.BlockSpec(shape, lambda *_: (0,) * len(shape), pipeline_mode=pl.Buffered(1))


def _rmsnorm(x, g):
    return x * lax.rsqrt(jnp.mean(x * x, axis=-1, keepdims=True) + EPS) * g


def _swiglu_half_step(x, g_ref, wg_ref, wu_ref, wd_ref, act_ref):
    h = _rmsnorm(x, g_ref[...]).astype(BF16)
    d_ff = wg_ref.shape[1]
    for c in range(d_ff // FF_CHUNK):
        cols = slice(c * FF_CHUNK, (c + 1) * FF_CHUNK)
        gate = _matmul(h, wg_ref[:, cols])
        up = _matmul(h, wu_ref[:, cols])
        half_gate = 0.5 * gate
        act_ref[:, cols] = ((half_gate + half_gate * jnp.tanh(half_gate)) * up).astype(BF16)
    y = _matmul(act_ref[...], wd_ref[...])
    return x + 0.5 * y


def _ffn1_kernel(x_ref, g_ref, wg_ref, wu_ref, wd_ref, o_ref, act_ref):
    o_ref[...] = _swiglu_half_step(x_ref[...], g_ref, wg_ref, wu_ref, wd_ref, act_ref)


def _out_ffn2_kernel(x_ref, a_ref, b_ref, wo_ref, g_ref, wg_ref, wu_ref, wd_ref, gf_ref,
                     o_ref, act_ref, *, final_norm):
    wa = wo_ref[0:DIFF_WIDTH, :]
    wb = wo_ref[DIFF_WIDTH:DIFF_WIDTH + DIL_WIDTH, :]
    x = x_ref[...] + _matmul(a_ref[...], wa) + _matmul(b_ref[...], wb)
    y = _swiglu_half_step(x, g_ref, wg_ref, wu_ref, wd_ref, act_ref)
    if final_norm:
        y = _rmsnorm(y, gf_ref[...])
    o_ref[...] = y


def _rope_tables(pos_ref, inv_ref):
    half = ROPE_DIM // 2
    pos = pos_ref[...].astype(F32)
    pos_row = jnp.concatenate([pos[a:a + 1, :] for a in range(pos.shape[0])], axis=1)
    ang = inv_ref[...] * pos_row
    cos, sin = jnp.cos(ang), jnp.sin(ang)
    rows = pos_row.shape[1]
    one = jnp.ones((HEAD_DIM - ROPE_DIM, rows), F32)
    zero = jnp.zeros((HEAD_DIM - ROPE_DIM, rows), F32)
    zero_half = jnp.zeros((half, rows), F32)
    heads = LANES // HEAD_DIM
    table = lambda head_rows: jnp.concatenate(head_rows * heads, axis=0).T
    return (table([cos, cos, one]),
            table([-sin, zero_half, zero]),
            table([zero_half, sin, zero]))


def _in_proj_kernel(x_ref, pos_ref, g_ref, w_ref, inv_ref, o_ref, *, rope_scale):
    h = _rmsnorm(x_ref[...], g_ref[...]).astype(BF16)
    tables = {1.0: _rope_tables(pos_ref, inv_ref)}
    half = ROPE_DIM // 2
    for c, scale in enumerate(rope_scale):
        cols = slice(c * PROJ_CHUNK, (c + 1) * PROJ_CHUNK)
        p = _matmul(h, w_ref[:, cols])
        if scale is None:
            o_ref[:, cols] = p.astype(BF16)
            continue
        if scale not in tables:
            tables[scale] = tuple(t * scale for t in tables[1.0])
        cos, sin_lo, sin_hi = tables[scale]
        for s in range(PROJ_CHUNK // LANES):
            xs = p[:, s * LANES:(s + 1) * LANES]
            rot = (xs * cos
                   + pltpu.roll(xs, LANES - half, axis=1) * sin_lo
                   + pltpu.roll(xs, half, axis=1) * sin_hi)
            lo = c * PROJ_CHUNK + s * LANES
            o_ref[:, lo:lo + LANES] = rot.astype(BF16)


def _attention_groups(q_ref, k_ref, v_ref, bias_ref, o_ref, scratch, finalize):
    for g in range(q_ref.shape[1] // LANES):
        lanes = slice(g * LANES, (g + 1) * LANES)
        _attention_maps(q_ref, k_ref, v_ref, bias_ref, o_ref, lanes, scratch[2 * g],
                        scratch[2 * g + 1], finalize)


def _attention_maps(q_ref, k_ref, v_ref, bias_ref, o_ref, lanes, qt_scr, vt_scr, finalize):
    seq = q_ref.shape[0]
    width = bias_ref.shape[0]
    qt_scr[...] = q_ref[:, lanes].astype(F32).T.astype(BF16)
    vt_scr[...] = v_ref[:, lanes].astype(F32).T.astype(BF16)
    lo_rows = lax.broadcasted_iota(jnp.int32, (LANES, Q_TILE), 0) < HEAD_DIM
    for t in range(seq // Q_TILE):
        queries = slice(t * Q_TILE, (t + 1) * Q_TILE)
        qt = qt_scr[:, queries]
        outs = []
        for q_half in (jnp.where(lo_rows, qt, 0), jnp.where(lo_rows, 0, qt)):
            m = l = acc = None
            for j in range((t + 1) * Q_TILE // KEY_CHUNK):
                keys = slice(j * KEY_CHUNK, (j + 1) * KEY_CHUNK)
                skip = max(0, j * KEY_CHUNK - t * Q_TILE)
                live = slice(skip, Q_TILE)
                s = _matmul(k_ref[keys, lanes], q_half[:, live])
                first = width - Q_TILE + j * KEY_CHUNK - t * Q_TILE
                if first >= 0:
                    s = s + bias_ref[first:first + KEY_CHUNK, live]
                m_chunk = jnp.max(s, axis=0, keepdims=True)
                m_new = m_chunk if m is None else jnp.maximum(m[:, live], m_chunk)
                p = jnp.exp2(s - m_new)
                p_sum = p[0:SUBLANES]
                for r in range(SUBLANES, KEY_CHUNK, SUBLANES):
                    p_sum = p_sum + p[r:r + SUBLANES]
                pv = _matmul(vt_scr[:, keys], p.astype(BF16))
                if m is None:
                    m, l, acc = m_new, p_sum, pv
                else:
                    alpha = jnp.exp2(m[:, live] - m_new)
                    new = (m_new, alpha * l[:, live] + p_sum, alpha * acc[:, live] + pv)
                    if skip:
                        new = [jnp.concatenate([old[:, 0:skip], upd], axis=1)
                               for old, upd in zip((m, l, acc), new)]
                    m, l, acc = new
            outs.append(acc * (1.0 / jnp.sum(l, axis=0, keepdims=True)))
        o_ref[queries, lanes] = finalize(*outs).astype(BF16)


def _diff_attn_kernel(q_ref, k_ref, v_ref, bias_ref, lq1_ref, lk1_ref, lq2_ref, lk2_ref, gain_ref,
                      o_ref, *scratch, lambda_init):
    lam = (jnp.exp(jnp.sum(lq1_ref[...] * lk1_ref[...], axis=-1, keepdims=True))
           - jnp.exp(jnp.sum(lq2_ref[...] * lk2_ref[...], axis=-1, keepdims=True))
           + lambda_init)

    def finalize(lo, hi):
        return _rmsnorm((lo - lam * hi).T, gain_ref[...]) * (1.0 - lambda_init)

    _attention_groups(q_ref, k_ref, v_ref, bias_ref, o_ref, scratch, finalize)


def _dil_attn_kernel(q_ref, k_ref, v_ref, bias_ref, o_ref, *scratch):
    def finalize(lo, hi):
        return jnp.concatenate([lo[0:HEAD_DIM], hi[HEAD_DIM:LANES]], axis=0).T

    _attention_groups(q_ref, k_ref, v_ref, bias_ref, o_ref, scratch, finalize)


def _bias_table(width, branches):
    c = np.arange(width)[:, None]
    r = np.arange(Q_TILE)[None, :]
    d = r - c + width - Q_TILE
    count = np.zeros(d.shape, np.float64)
    for span, stride in branches:
        count += (d >= 0) & (d <= span) & (d % stride == 0)
    for first in range(width - Q_TILE, -1, -Q_TILE):
        assert (count[first:first + KEY_CHUNK] > 0).any(axis=0).all()
    with np.errstate(divide="ignore"):
        return jnp.asarray(np.log2(count), dtype=F32)


def _row_spec(cols, rows=ROW_TILE):
    return pl.BlockSpec((rows, cols), lambda i: (i, 0))


def _params(*semantics):
    return pltpu.CompilerParams(dimension_semantics=semantics, vmem_limit_bytes=VMEM_LIMIT)


def _ffn1(x, g, wg, wu, wd):
    t, d = x.shape
    f = wg.shape[1]
    return pl.pallas_call(
        _ffn1_kernel,
        grid=(t // WIDE_ROW_TILE,),
        in_specs=[_row_spec(d, WIDE_ROW_TILE), _resident((1, d)), _resident((d, f)),
                  _resident((d, f)), _resident((f, d))],
        out_specs=_row_spec(d, WIDE_ROW_TILE),
        out_shape=jax.ShapeDtypeStruct((t, d), F32),
        scratch_shapes=[pltpu.VMEM((WIDE_ROW_TILE, f), BF16)],
        compiler_params=_params("parallel"),
        name="ffn1",
    )(x, g, wg, wu, wd)


def _in_proj(x, pos, g, w, rope_scale):
    t, d = x.shape
    n = w.shape[1]
    half = ROPE_DIM // 2
    inv = jnp.exp(-math.log(ROPE_THETA) * jnp.arange(half, dtype=F32) * 2.0 / ROPE_DIM)
    rows = WIDE_ROW_TILE
    pos_tiles = pos.reshape(t // rows, rows // LANES, LANES)
    return pl.pallas_call(
        functools.partial(_in_proj_kernel, rope_scale=rope_scale),
        grid=(t // rows,),
        in_specs=[_row_spec(d, rows),
                  pl.BlockSpec((None, rows // LANES, LANES), lambda i: (i, 0, 0)),
                  _resident((1, d)), _resident((d, n)), _resident((half, 1))],
        out_specs=_row_spec(n, rows),
        out_shape=jax.ShapeDtypeStruct((t, n), BF16),
        compiler_params=_params("parallel"),
        name="in_proj",
    )(x, pos_tiles, g, w, inv.reshape(half, 1))


def _attention(body, name, proj, bias, extra, *, q_col, k_col, v_col, groups):
    b, s, _ = proj.shape
    assert groups % ATTN_GROUPS == 0 and all(c % ATTN_GROUPS == 0 for c in (q_col, k_col, v_col))
    cols = ATTN_GROUPS * LANES
    col_spec = lambda first: pl.BlockSpec((None, s, cols),
                                          lambda bi, gi: (bi, 0, first // ATTN_GROUPS + gi))
    return pl.pallas_call(
        body,
        grid=(b, groups // ATTN_GROUPS),
        in_specs=[col_spec(q_col), col_spec(k_col), col_spec(v_col), _resident(bias.shape)]
                 + [_resident(e.shape) for e in extra],
        out_specs=pl.BlockSpec((None, s, cols), lambda bi, gi: (bi, 0, gi)),
        out_shape=jax.ShapeDtypeStruct((b, s, groups * LANES), BF16),
        scratch_shapes=[pltpu.VMEM((LANES, s), BF16)] * (2 * ATTN_GROUPS),
        compiler_params=_params("parallel", "parallel"),
        name=name,
    )(proj, proj, proj, bias, *extra)


def _out_ffn2(x, a, b, wo, g, wg, wu, wd, gf, final_norm):
    t, d = x.shape
    f = wg.shape[1]
    return pl.pallas_call(
        functools.partial(_out_ffn2_kernel, final_norm=final_norm),
        grid=(t // ROW_TILE,),
        in_specs=[_row_spec(d), _row_spec(a.shape[1]), _row_spec(b.shape[1]),
                  _resident(wo.shape), _resident((1, d)), _resident((d, f)), _resident((d, f)),
                  _resident((f, d)), _resident((1, d))],
        out_specs=_row_spec(d),
        out_shape=jax.ShapeDtypeStruct((t, d), F32),
        scratch_shapes=[pltpu.VMEM((ROW_TILE, f), BF16)],
        compiler_params=_params("parallel"),
        name="out_ffn2",
    )(x, a, b, wo, g, wg, wu, wd, gf)


def kernel(x, positions, ffn1_norm, ffn1_gate, ffn1_up, ffn1_down, mix_norm, w_in,
           lambda_q1, lambda_k1, lambda_q2, lambda_k2, subln_gain, w_out,
           ffn2_norm, ffn2_gate, ffn2_up, ffn2_down, final_norm):
    bsz, seq, d = x.shape
    depth = ffn1_norm.shape[0]
    t = bsz * seq
    assert t % WIDE_ROW_TILE == 0 and WIDE_ROW_TILE % ROW_TILE == 0
    assert seq % Q_TILE == 0 and Q_TILE % KEY_CHUNK == 0
    assert w_in.shape[2] == 3 * (DIFF_WIDTH + DIL_WIDTH) and DIFF_WIDTH % PROJ_CHUNK == 0

    row = lambda v: v.astype(F32).reshape(1, -1)
    q_scale = HEAD_DIM ** -0.5 * math.log2(math.e)
    rope_scale = (q_scale, 1.0, None, q_scale, 1.0, None)
    rope_scale = tuple(s for s in rope_scale for _ in range(DIFF_WIDTH // PROJ_CHUNK))
    groups = DIFF_WIDTH // LANES
    causal_bias = _bias_table(Q_TILE, ((seq, 1),))
    dilated_bias = _bias_table(seq, DIL_CONFIGS)
    pos = positions.reshape(t)

    h = x.reshape(t, d)
    for l in range(depth):
        lambda_init = 0.8 - 0.6 * math.exp(-0.3 * l)
        h = _ffn1(h, row(ffn1_norm[l]), ffn1_gate[l], ffn1_up[l], ffn1_down[l])
        proj = _in_proj(h, pos, row(mix_norm[l]), w_in[l], rope_scale)
        proj = proj.reshape(bsz, seq, -1)
        lams = (row(lambda_q1[l]), row(lambda_k1[l]), row(lambda_q2[l]), row(lambda_k2[l]))
        a = _attention(functools.partial(_diff_attn_kernel, lambda_init=lambda_init), "diff_attn",
                       proj, causal_bias, (*lams, row(subln_gain[l])),
                       q_col=0, k_col=groups, v_col=2 * groups, groups=groups)
        b = _attention(_dil_attn_kernel, "dil_attn", proj, dilated_bias, (),
                       q_col=3 * groups, k_col=4 * groups, v_col=5 * groups, groups=groups)
        h = _out_ffn2(h, a.reshape(t, -1), b.reshape(t, -1), w_out[l], row(ffn2_norm[l]),
                      ffn2_gate[l], ffn2_up[l], ffn2_down[l], row(final_norm),
                      final_norm=(l == depth - 1))
    return h.reshape(bsz, seq, d)
```

```python
import functools
import math

import numpy as np
import jax
import jax.numpy as jnp
from jax import lax
from jax.experimental import pallas as pl
from jax.experimental.pallas import tpu as pltpu

HEAD_DIM = 64
DIFF_HEADS = 4
DIFF_VDIM = 2 * HEAD_DIM
DIL_HEADS = 8
DIL_CONFIGS = ((128, 1), (512, 4), (2048, 16))
DIFF_WIDTH = DIFF_HEADS * DIFF_VDIM
DIL_WIDTH = DIL_HEADS * HEAD_DIM
ROPE_THETA = 500000.0
ROPE_DIM = HEAD_DIM // 4
EPS = 1e-5

LANES = 128
SUBLANES = 8
ROW_TILE = 512
WIDE_ROW_TILE = 1024
FF_CHUNK = 256
PROJ_CHUNK = 512
Q_TILE = 256
KEY_CHUNK = 128
ATTN_GROUPS = 2
VMEM_LIMIT = 60 * 1024 * 1024

F32 = jnp.float32
BF16 = jnp.bfloat16
MATMUL_DIMS = (((1,), (0,)), ((), ()))


def _matmul(a, b):
    return lax.dot_general(a, b, MATMUL_DIMS, preferred_element_type=F32)


def _resident(shape):
    return pl.BlockSpec(shape, lambda *_: (0,) * len(shape), pipeline_mode=pl.Buffered(1))


def _rmsnorm(x, g):
    return x * lax.rsqrt(jnp.mean(x * x, axis=-1, keepdims=True) + EPS) * g


def _swiglu_half_step(x, g_ref, wg_ref, wu_ref, wd_ref, act_ref):
    h = _rmsnorm(x, g_ref[...]).astype(BF16)
    d_ff = wg_ref.shape[1]
    for c in range(d_ff // FF_CHUNK):
        cols = slice(c * FF_CHUNK, (c + 1) * FF_CHUNK)
        gate = _matmul(h, wg_ref[:, cols])
        up = _matmul(h, wu_ref[:, cols])
        act_ref[:, cols] = (gate * jax.nn.sigmoid(gate) * up).astype(BF16)
    y = _matmul(act_ref[...], wd_ref[...])
    return x + 0.5 * y


def _ffn1_kernel(x_ref, g_ref, wg_ref, wu_ref, wd_ref, o_ref, act_ref):
    o_ref[...] = _swiglu_half_step(x_ref[...], g_ref, wg_ref, wu_ref, wd_ref, act_ref)


def _out_ffn2_kernel(x_ref, a_ref, b_ref, wo_ref, g_ref, wg_ref, wu_ref, wd_ref, gf_ref,
                     o_ref, act_ref, *, final_norm):
    wa = wo_ref[0:DIFF_WIDTH, :]
    wb = wo_ref[DIFF_WIDTH:DIFF_WIDTH + DIL_WIDTH, :]
    x = x_ref[...] + _matmul(a_ref[...], wa) + _matmul(b_ref[...], wb)
    y = _swiglu_half_step(x, g_ref, wg_ref, wu_ref, wd_ref, act_ref)
    if final_norm:
        y = _rmsnorm(y, gf_ref[...])
    o_ref[...] = y


def _rope_tables(pos_ref, inv_ref):
    half = ROPE_DIM // 2
    pos = pos_ref[...].astype(F32)
    pos_row = jnp.concatenate([pos[a:a + 1, :] for a in range(pos.shape[0])], axis=1)
    ang = inv_ref[...] * pos_row
    cos, sin = jnp.cos(ang), jnp.sin(ang)
    rows = pos_row.shape[1]
    one = jnp.ones((HEAD_DIM - ROPE_DIM, rows), F32)
    zero = jnp.zeros((HEAD_DIM - ROPE_DIM, rows), F32)
    zero_half = jnp.zeros((half, rows), F32)
    heads = LANES // HEAD_DIM
    table = lambda head_rows: jnp.concatenate(head_rows * heads, axis=0).T
    return (table([cos, cos, one]),
            table([-sin, zero_half, zero]),
            table([zero_half, sin, zero]))


def _in_proj_kernel(x_ref, pos_ref, g_ref, w_ref, inv_ref, o_ref, *, rope_scale):
    h = _rmsnorm(x_ref[...], g_ref[...]).astype(BF16)
    tables = {1.0: _rope_tables(pos_ref, inv_ref)}
    half = ROPE_DIM // 2
    for c, scale in enumerate(rope_scale):
        cols = slice(c * PROJ_CHUNK, (c + 1) * PROJ_CHUNK)
        p = _matmul(h, w_ref[:, cols])
        if scale is None:
            o_ref[:, cols] = p.astype(BF16)
            continue
        if scale not in tables:
            tables[scale] = tuple(t * scale for t in tables[1.0])
        cos, sin_lo, sin_hi = tables[scale]
        for s in range(PROJ_CHUNK // LANES):
            xs = p[:, s * LANES:(s + 1) * LANES]
            rot = (xs * cos
                   + pltpu.roll(xs, LANES - half, axis=1) * sin_lo
                   + pltpu.roll(xs, half, axis=1) * sin_hi)
            lo = c * PROJ_CHUNK + s * LANES
            o_ref[:, lo:lo + LANES] = rot.astype(BF16)


def _attention_groups(q_ref, k_ref, v_ref, bias_ref, o_ref, scratch, finalize):
    for g in range(q_ref.shape[1] // LANES):
        lanes = slice(g * LANES, (g + 1) * LANES)
        _attention_maps(q_ref, k_ref, v_ref, bias_ref, o_ref, lanes, scratch[2 * g],
                        scratch[2 * g + 1], finalize)


def _attention_maps(q_ref, k_ref, v_ref, bias_ref, o_ref, lanes, qt_scr, vt_scr, finalize):
    seq = q_ref.shape[0]
    width = bias_ref.shape[0]
    qt_scr[...] = q_ref[:, lanes].astype(F32).T.astype(BF16)
    vt_scr[...] = v_ref[:, lanes].astype(F32).T.astype(BF16)
    lo_rows = lax.broadcasted_iota(jnp.int32, (LANES, Q_TILE), 0) < HEAD_DIM
    for t in reversed(range(seq // Q_TILE)):
        queries = slice(t * Q_TILE, (t + 1) * Q_TILE)
        qt = qt_scr[:, queries]
        outs = []
        for q_half in (jnp.where(lo_rows, qt, 0), jnp.where(lo_rows, 0, qt)):
            m = l = acc = None
            for j in range((t + 1) * Q_TILE // KEY_CHUNK):
                keys = slice(j * KEY_CHUNK, (j + 1) * KEY_CHUNK)
                skip = max(0, j * KEY_CHUNK - t * Q_TILE)
                live = slice(skip, Q_TILE)
                s = _matmul(k_ref[keys, lanes], q_half[:, live])
                first = width - Q_TILE + j * KEY_CHUNK - t * Q_TILE
                if first >= 0:
                    s = s + bias_ref[first:first + KEY_CHUNK, live]
                m_chunk = jnp.max(s, axis=0, keepdims=True)
                m_new = m_chunk if m is None else jnp.maximum(m[:, live], m_chunk)
                p = jnp.exp2(s - m_new)
                p_sum = p[0:SUBLANES]
                for r in range(SUBLANES, KEY_CHUNK, SUBLANES):
                    p_sum = p_sum + p[r:r + SUBLANES]
                pv = _matmul(vt_scr[:, keys], p.astype(BF16))
                if m is None:
                    m, l, acc = m_new, p_sum, pv
                else:
                    alpha = jnp.exp2(m[:, live] - m_new)
                    new = (m_new, alpha * l[:, live] + p_sum, alpha * acc[:, live] + pv)
                    if skip:
                        new = [jnp.concatenate([old[:, 0:skip], upd], axis=1)
                               for old, upd in zip((m, l, acc), new)]
                    m, l, acc = new
            outs.append(acc * (1.0 / jnp.sum(l, axis=0, keepdims=True)))
        o_ref[queries, lanes] = finalize(*outs).astype(BF16)


def _diff_attn_kernel(q_ref, k_ref, v_ref, bias_ref, lq1_ref, lk1_ref, lq2_ref, lk2_ref, gain_ref,
                      o_ref, *scratch, lambda_init):
    lam = (jnp.exp(jnp.sum(lq1_ref[...] * lk1_ref[...], axis=-1, keepdims=True))
           - jnp.exp(jnp.sum(lq2_ref[...] * lk2_ref[...], axis=-1, keepdims=True))
           + lambda_init)

    def finalize(lo, hi):
        return _rmsnorm((lo - lam * hi).T, gain_ref[...]) * (1.0 - lambda_init)

    _attention_groups(q_ref, k_ref, v_ref, bias_ref, o_ref, scratch, finalize)


def _dil_attn_kernel(q_ref, k_ref, v_ref, bias_ref, o_ref, *scratch):
    def finalize(lo, hi):
        return jnp.concatenate([lo[0:HEAD_DIM], hi[HEAD_DIM:LANES]], axis=0).T

    _attention_groups(q_ref, k_ref, v_ref, bias_ref, o_ref, scratch, finalize)


def _bias_table(width, branches):
    c = np.arange(width)[:, None]
    r = np.arange(Q_TILE)[None, :]
    d = r - c + width - Q_TILE
    count = np.zeros(d.shape, np.float64)
    for span, stride in branches:
        count += (d >= 0) & (d <= span) & (d % stride == 0)
    for first in range(width - Q_TILE, -1, -Q_TILE):
        assert (count[first:first + KEY_CHUNK] > 0).any(axis=0).all()
    with np.errstate(divide="ignore"):
        return jnp.asarray(np.log2(count), dtype=F32)


def _row_spec(cols, rows=ROW_TILE):
    return pl.BlockSpec((rows, cols), lambda i: (i, 0))


def _params(*semantics):
    return pltpu.CompilerParams(dimension_semantics=semantics, vmem_limit_bytes=VMEM_LIMIT)


def _ffn1(x, g, wg, wu, wd):
    t, d = x.shape
    f = wg.shape[1]
    return pl.pallas_call(
        _ffn1_kernel,
        grid=(t // WIDE_ROW_TILE,),
        in_specs=[_row_spec(d, WIDE_ROW_TILE), _resident((1, d)), _resident((d, f)),
                  _resident((d, f)), _resident((f, d))],
        out_specs=_row_spec(d, WIDE_ROW_TILE),
        out_shape=jax.ShapeDtypeStruct((t, d), F32),
        scratch_shapes=[pltpu.VMEM((WIDE_ROW_TILE, f), BF16)],
        compiler_params=_params("parallel"),
        name="ffn1",
    )(x, g, wg, wu, wd)


def _in_proj(x, pos, g, w, rope_scale):
    t, d = x.shape
    n = w.shape[1]
    half = ROPE_DIM // 2
    inv = jnp.exp(-math.log(ROPE_THETA) * jnp.arange(half, dtype=F32) * 2.0 / ROPE_DIM)
    rows = WIDE_ROW_TILE
    pos_tiles = pos.reshape(t // rows, rows // LANES, LANES)
    return pl.pallas_call(
        functools.partial(_in_proj_kernel, rope_scale=rope_scale),
        grid=(t // rows,),
        in_specs=[_row_spec(d, rows),
                  pl.BlockSpec((None, rows // LANES, LANES), lambda i: (i, 0, 0)),
                  _resident((1, d)), _resident((d, n)), _resident((half, 1))],
        out_specs=_row_spec(n, rows),
        out_shape=jax.ShapeDtypeStruct((t, n), BF16),
        compiler_params=_params("parallel"),
        name="in_proj",
    )(x, pos_tiles, g, w, inv.reshape(half, 1))


def _attention(body, name, proj, bias, extra, *, q_col, k_col, v_col, groups):
    b, s, _ = proj.shape
    assert groups % ATTN_GROUPS == 0 and all(c % ATTN_GROUPS == 0 for c in (q_col, k_col, v_col))
    cols = ATTN_GROUPS * LANES
    col_spec = lambda first: pl.BlockSpec((None, s, cols),
                                          lambda bi, gi: (bi, 0, first // ATTN_GROUPS + gi))
    return pl.pallas_call(
        body,
        grid=(b, groups // ATTN_GROUPS),
        in_specs=[col_spec(q_col), col_spec(k_col), col_spec(v_col), _resident(bias.shape)]
                 + [_resident(e.shape) for e in extra],
        out_specs=pl.BlockSpec((None, s, cols), lambda bi, gi: (bi, 0, gi)),
        out_shape=jax.ShapeDtypeStruct((b, s, groups * LANES), BF16),
        scratch_shapes=[pltpu.VMEM((LANES, s), BF16)] * (2 * ATTN_GROUPS),
        compiler_params=_params("parallel", "parallel"),
        name=name,
    )(proj, proj, proj, bias, *extra)


def _out_ffn2(x, a, b, wo, g, wg, wu, wd, gf, final_norm):
    t, d = x.shape
    f = wg.shape[1]
    return pl.pallas_call(
        functools.partial(_out_ffn2_kernel, final_norm=final_norm),
        grid=(t // ROW_TILE,),
        in_specs=[_row_spec(d), _row_spec(a.shape[1]), _row_spec(b.shape[1]),
                  _resident(wo.shape), _resident((1, d)), _resident((d, f)), _resident((d, f)),
                  _resident((f, d)), _resident((1, d))],
        out_specs=_row_spec(d),
        out_shape=jax.ShapeDtypeStruct((t, d), F32),
        scratch_shapes=[pltpu.VMEM((ROW_TILE, f), BF16)],
        compiler_params=_params("parallel"),
        name="out_ffn2",
    )(x, a, b, wo, g, wg, wu, wd, gf)


def kernel(x, positions, ffn1_norm, ffn1_gate, ffn1_up, ffn1_down, mix_norm, w_in,
           lambda_q1, lambda_k1, lambda_q2, lambda_k2, subln_gain, w_out,
           ffn2_norm, ffn2_gate, ffn2_up, ffn2_down, final_norm):
    bsz, seq, d = x.shape
    depth = ffn1_norm.shape[0]
    t = bsz * seq
    assert t % WIDE_ROW_TILE == 0 and WIDE_ROW_TILE % ROW_TILE == 0
    assert seq % Q_TILE == 0 and Q_TILE % KEY_CHUNK == 0
    assert w_in.shape[2] == 3 * (DIFF_WIDTH + DIL_WIDTH) and DIFF_WIDTH % PROJ_CHUNK == 0

    row = lambda v: v.astype(F32).reshape(1, -1)
    q_scale = HEAD_DIM ** -0.5 * math.log2(math.e)
    rope_scale = (q_scale, 1.0, None, q_scale, 1.0, None)
    rope_scale = tuple(s for s in rope_scale for _ in range(DIFF_WIDTH // PROJ_CHUNK))
    groups = DIFF_WIDTH // LANES
    causal_bias = _bias_table(Q_TILE, ((seq, 1),))
    dilated_bias = _bias_table(seq, DIL_CONFIGS)
    pos = positions.reshape(t)

    h = x.reshape(t, d)
    for l in range(depth):
        lambda_init = 0.8 - 0.6 * math.exp(-0.3 * l)
        h = _ffn1(h, row(ffn1_norm[l]), ffn1_gate[l], ffn1_up[l], ffn1_down[l])
        proj = _in_proj(h, pos, row(mix_norm[l]), w_in[l], rope_scale)
        proj = proj.reshape(bsz, seq, -1)
        lams = (row(lambda_q1[l]), row(lambda_k1[l]), row(lambda_q2[l]), row(lambda_k2[l]))
        a = _attention(functools.partial(_diff_attn_kernel, lambda_init=lambda_init), "diff_attn",
                       proj, causal_bias, (*lams, row(subln_gain[l])),
                       q_col=0, k_col=groups, v_col=2 * groups, groups=groups)
        b = _attention(_dil_attn_kernel, "dil_attn", proj, dilated_bias, (),
                       q_col=3 * groups, k_col=4 * groups, v_col=5 * groups, groups=groups)
        h = _out_ffn2(h, a.reshape(t, -1), b.reshape(t, -1), w_out[l], row(ffn2_norm[l]),
                      ffn2_gate[l], ffn2_up[l], ffn2_down[l], row(final_norm),
                      final_norm=(l == depth - 1))
    return h.reshape(bsz, seq, d)
```
